```python
import math
import jax, jax.numpy as jnp
from jax import lax
import numpy as np

D_MODEL = 2048
BATCH = 8
SEQ = 4096
DEPTH = 1
DEC_BATCH = 16
DEC_SEQ = 32
PAST_LEN = 1024

CHUNK = 64
Q_BLOCK = 128
EPS = 1e-6
GDN_HK = 16
GDN_HV = 32
GDN_DK = 128
GDN_DV = 128
GDN_CONV = 4
GDN_QK_DIM = GDN_HK * GDN_DK
GDN_V_DIM = GDN_HV * GDN_DV
GDN_CONV_DIM = 2 * GDN_QK_DIM + GDN_V_DIM
DIFF_H = 8
DIFF_DK = 128
DIFF_DV = 2 * DIFF_DK
DIFF_QK_DIM = DIFF_H * 2 * DIFF_DK
DIFF_V_DIM = DIFF_H * DIFF_DV
ROPE_THETA = 500000.0
ROT_DIM = DIFF_DK // 4
D_FF = 5632
FFN_CONV = 3
IN_WIDTHS = (GDN_QK_DIM, GDN_QK_DIM, GDN_V_DIM, GDN_V_DIM, GDN_HV, GDN_HV,
             DIFF_QK_DIM, DIFF_QK_DIM, DIFF_V_DIM, D_MODEL, D_MODEL)
D_IN = sum(IN_WIDTHS)

kernel_name = 'hybrid_gdn_diffattn_streaming_step'


def rms_norm(x, w):
    xf = x.astype(jnp.float32)
    y = xf * lax.rsqrt(jnp.mean(xf * xf, axis=-1, keepdims=True) + EPS)
    return (y * w.astype(jnp.float32)).astype(x.dtype)


def l2norm(x):
    xf = x.astype(jnp.float32)
    return xf * lax.rsqrt(jnp.sum(xf * xf, axis=-1, keepdims=True) + EPS)


def causal_dwconv(x_ext, w):
    c = w.shape[1]
    return lax.conv_general_dilated(x_ext, w[:, None, :].astype(x_ext.dtype), window_strides=(1,),
                                    padding='VALID', dimension_numbers=('NWC', 'WIO', 'NWC'),
                                    feature_group_count=c)


def in_projection(xn, w_in):
    segs = []
    start = 0
    for width in IN_WIDTHS:
        segs.append(xn @ w_in[:, start:start + width])
        start += width
    return segs


def rope_partial(x, pos):
    half = ROT_DIM // 2
    inv = jnp.exp(-math.log(ROPE_THETA) * jnp.arange(0, ROT_DIM, 2, dtype=jnp.float32) / ROT_DIM)
    ang = pos.astype(jnp.float32)[:, None] * inv[None, :]
    cos = jnp.cos(ang)[None, :, None, None, :]
    sin = jnp.sin(ang)[None, :, None, None, :]
    xr = x[..., :ROT_DIM].astype(jnp.float32)
    x1, x2 = xr[..., :half], xr[..., half:]
    rot = jnp.concatenate([x1 * cos - x2 * sin, x2 * cos + x1 * sin], axis=-1)
    return jnp.concatenate([rot.astype(x.dtype), x[..., ROT_DIM:]], axis=-1)


def gated_delta_chunked(q, k, v, g, beta, s0, chunk):
    B, T, H, DK = q.shape
    DV = v.shape[-1]
    N = T // chunk

    def blocks(a):
        a = a.reshape((B, N, chunk, H) + a.shape[3:])
        return jnp.moveaxis(a, 3, 1)

    q, k, v, g, beta = blocks(q), blocks(k), blocks(v), blocks(g), blocks(beta)
    G = jnp.cumsum(g, axis=-1)
    causal = jnp.tril(jnp.ones((chunk, chunk), bool))
    strict = jnp.tril(jnp.ones((chunk, chunk), bool), -1)
    gdiff = G[..., :, None] - G[..., None, :]
    decay = jnp.where(causal, jnp.exp(jnp.where(causal, gdiff, 0.0)), 0.0)
    kb = k * beta[..., None]
    lower = jnp.where(strict, jnp.einsum('bhncd,bhnmd->bhncm', kb, k) * decay, 0.0)
    tri = lower + jnp.eye(chunk, dtype=lower.dtype)
    u = lax.linalg.triangular_solve(tri, v * beta[..., None], left_side=True, lower=True,
                                    unit_diagonal=True)
    w = lax.linalg.triangular_solve(tri, kb * jnp.exp(G)[..., None], left_side=True, lower=True,
                                    unit_diagonal=True)
    intra = jnp.einsum('bhncd,bhnmd->bhncm', q, k) * decay
    qg = q * jnp.exp(G)[..., None]
    kd = k * jnp.exp(G[..., -1:] - G)[..., None]
    gl = jnp.exp(G[..., -1])

    def step(s, xs):
        u_c, w_c, a_c, qg_c, kd_c, gl_c = xs
        v_new = u_c - jnp.einsum('bhcd,bhde->bhce', w_c, s)
        o_c = jnp.einsum('bhcd,bhde->bhce', qg_c, s) + jnp.einsum('bhcm,bhme->bhce', a_c, v_new)
        s = s * gl_c[..., None, None] + jnp.einsum('bhcd,bhce->bhde', kd_c, v_new)
        return s, o_c

    xs = (jnp.moveaxis(u, 2, 0), jnp.moveaxis(w, 2, 0), jnp.moveaxis(intra, 2, 0),
          jnp.moveaxis(qg, 2, 0), jnp.moveaxis(kd, 2, 0), jnp.moveaxis(gl, 2, 0))
    s_final, o = lax.scan(step, s0, xs)
    o = jnp.transpose(o, (1, 0, 3, 2, 4)).reshape(B, T, H, DV)
    return o, s_final


def gdn_branch(gq, gk, gv, gz, ga, gb, conv_hist, s0, conv_w, a_log, dt_bias, norm_w):
    B, T, _ = gq.shape
    qkv = jnp.concatenate([gq, gk, gv], axis=-1)
    qkv_ext = jnp.concatenate([conv_hist.astype(qkv.dtype), qkv], axis=1)
    conv_new = qkv_ext[:, -(GDN_CONV - 1):]
    c = jax.nn.silu(causal_dwconv(qkv_ext, conv_w))
    q = c[..., :GDN_QK_DIM].reshape(B, T, GDN_HK, GDN_DK)
    k = c[..., GDN_QK_DIM:2 * GDN_QK_DIM].reshape(B, T, GDN_HK, GDN_DK)
    v = c[..., 2 * GDN_QK_DIM:].reshape(B, T, GDN_HV, GDN_DV).astype(jnp.float32)
    rep = GDN_HV // GDN_HK
    q = jnp.repeat(l2norm(q) * (GDN_DK ** -0.5), rep, axis=2)
    k = jnp.repeat(l2norm(k), rep, axis=2)
    beta = jax.nn.sigmoid(gb.astype(jnp.float32))
    g = -jnp.exp(a_log.astype(jnp.float32)) * jax.nn.softplus(ga.astype(jnp.float32) + dt_bias.astype(jnp.float32))
    chunk = CHUNK if T % CHUNK == 0 else T
    o, s_new = gated_delta_chunked(q, k, v, g, beta, s0.astype(jnp.float32), chunk)
    o = rms_norm(o, norm_w) * jax.nn.silu(gz.reshape(B, T, GDN_HV, GDN_DV).astype(jnp.float32))
    return o.reshape(B, T, GDN_V_DIM).astype(gq.dtype), conv_new, s_new.astype(s0.dtype)


def diff_attend(q, qpos, K, V, kpos, lam):
    s = jnp.einsum('bqhcd,bkhcd->bhcqk', q.astype(jnp.float32), K.astype(jnp.float32)) * (DIFF_DK ** -0.5)
    visible = (kpos // CHUNK)[None, :] <= (qpos // CHUNK)[:, None]
    s = jnp.where(visible, s, jnp.finfo(jnp.float32).min)
    p = jax.nn.softmax(s, axis=-1)
    wts = p[:, :, 0] - lam * p[:, :, 1]
    return jnp.einsum('bhqk,bkhe->bqhe', wts, V.astype(jnp.float32))


def diff_branch(dq, dk, dv, k_hist, v_hist, pos_offset, lam_q1, lam_k1, lam_q2, lam_k2, subln_w, lam_init):
    B, T, _ = dq.shape
    P = k_hist.shape[1]
    qpos = pos_offset + jnp.arange(T, dtype=jnp.int32)
    q = rope_partial(dq.reshape(B, T, DIFF_H, 2, DIFF_DK), qpos)
    k = rope_partial(dk.reshape(B, T, DIFF_H, 2, DIFF_DK), qpos)
    k_rows = k.reshape(B, T, DIFF_H, 2 * DIFF_DK)
    v_rows = dv.reshape(B, T, DIFF_H, DIFF_DV)
    K = jnp.concatenate([k_hist.astype(k_rows.dtype), k_rows], axis=1).reshape(B, P + T, DIFF_H, 2, DIFF_DK)
    V = jnp.concatenate([v_hist.astype(v_rows.dtype), v_rows], axis=1)
    kpos = jnp.arange(P + T, dtype=jnp.int32)
    lam = (jnp.exp(jnp.sum(lam_q1.astype(jnp.float32) * lam_k1.astype(jnp.float32)))
           - jnp.exp(jnp.sum(lam_q2.astype(jnp.float32) * lam_k2.astype(jnp.float32))) + lam_init)
    if T > Q_BLOCK and T % Q_BLOCK == 0:
        nb = T // Q_BLOCK
        qb = jnp.moveaxis(q.reshape(B, nb, Q_BLOCK, DIFF_H, 2, DIFF_DK), 1, 0)
        pb = qpos.reshape(nb, Q_BLOCK)
        o = lax.map(lambda a: diff_attend(a[0], a[1], K, V, kpos, lam), (qb, pb))
        o = jnp.moveaxis(o, 0, 1).reshape(B, T, DIFF_H, DIFF_DV)
    else:
        o = diff_attend(q, qpos, K, V, kpos, lam)
    o = rms_norm(o.astype(dq.dtype), subln_w) * (1.0 - lam_init)
    return o.reshape(B, T, DIFF_V_DIM), k_rows, v_rows


def conv_ffn(hn, hist, w_gate, w_up, conv_w, conv_b, w_down):
    g = hn @ w_gate
    g_ext = jnp.concatenate([hist.astype(g.dtype), g], axis=1)
    new_hist = g_ext[:, -(FFN_CONV - 1):]
    g = causal_dwconv(g_ext, conv_w) + conv_b
    return (jax.nn.silu(g) * (hn @ w_up)) @ w_down, new_hist


def layer(x, pos_offset, k_hist, v_hist, gdn_conv_hist, gdn_s0, ffn_hist, lam_init,
          norm_mix, w_in, gdn_conv_w, gdn_a_log, gdn_dt_bias, gdn_norm_w,
          lam_q1, lam_k1, lam_q2, lam_k2, subln_w, w_branch, w_out,
          norm_ffn, w_gate, w_up, ffn_conv_w, ffn_conv_b, w_down):
    xn = rms_norm(x, norm_mix)
    gq, gk, gv, gz, ga, gb, dq, dk, dv, gate_a, gate_b = in_projection(xn, w_in)
    ya, gdn_conv_new, gdn_s = gdn_branch(gq, gk, gv, gz, ga, gb, gdn_conv_hist, gdn_s0,
                                         gdn_conv_w, gdn_a_log, gdn_dt_bias, gdn_norm_w)
    yb, k_rows, v_rows = diff_branch(dq, dk, dv, k_hist, v_hist, pos_offset,
                                     lam_q1, lam_k1, lam_q2, lam_k2, subln_w, lam_init)
    merged = (jax.nn.sigmoid(gate_a) * (ya @ w_branch[:GDN_V_DIM])
              + jax.nn.sigmoid(gate_b) * (yb @ w_branch[GDN_V_DIM:]))
    h = x + merged @ w_out
    f, ffn_new = conv_ffn(rms_norm(h, norm_ffn), ffn_hist, w_gate, w_up, ffn_conv_w, ffn_conv_b, w_down)
    return h + f, k_rows, v_rows, gdn_conv_new, gdn_s, ffn_new


def setup_inputs(seed: int = 0) -> dict:
    key = jax.random.key(seed)
    ks = jax.random.split(key, 32)
    f32 = jnp.float32

    def nrm(k, shape, scale):
        return jax.random.normal(k, shape, f32) * scale

    dt = jnp.exp(jax.random.uniform(ks[10], (DEPTH, GDN_HV), f32, math.log(1e-3), math.log(1e-1)))
    return {
        'x_prompt': nrm(ks[0], (BATCH, SEQ, D_MODEL), 1.0),
        'x_sample': nrm(ks[1], (DEC_BATCH, DEC_SEQ, D_MODEL), 1.0),
        'cache_diff_k': nrm(ks[2], (DEPTH, DEC_BATCH, PAST_LEN, DIFF_H, 2 * DIFF_DK), 1.0),
        'cache_diff_v': nrm(ks[3], (DEPTH, DEC_BATCH, PAST_LEN, DIFF_H, DIFF_DV), 1.0),
        'state_gdn_conv': nrm(ks[4], (DEPTH, DEC_BATCH, GDN_CONV - 1, GDN_CONV_DIM), 1.0),
        'state_gdn_rec': nrm(ks[5], (DEPTH, DEC_BATCH, GDN_HV, GDN_DK, GDN_DV), GDN_DK ** -0.5),
        'state_ffn_conv': nrm(ks[6], (DEPTH, DEC_BATCH, FFN_CONV - 1, D_FF), 1.0),
        'norm_mix': 1.0 + nrm(ks[7], (DEPTH, D_MODEL), 0.02),
        'w_in': nrm(ks[8], (DEPTH, D_MODEL, D_IN), D_MODEL ** -0.5),
        'gdn_conv_w': nrm(ks[9], (DEPTH, GDN_CONV, GDN_CONV_DIM), GDN_CONV ** -0.5),
        'gdn_a_log': jnp.log(jax.random.uniform(ks[11], (DEPTH, GDN_HV), f32, 1.0, 16.0)),
        'gdn_dt_bias': dt + jnp.log(-jnp.expm1(-dt)),
        'gdn_norm_w': 1.0 + nrm(ks[12], (DEPTH, GDN_DV), 0.02),
        'diff_lambda_q1': nrm(ks[13], (DEPTH, DIFF_DK), 0.1),
        'diff_lambda_k1': nrm(ks[14], (DEPTH, DIFF_DK), 0.1),
        'diff_lambda_q2': nrm(ks[15], (DEPTH, DIFF_DK), 0.1),
        'diff_lambda_k2': nrm(ks[16], (DEPTH, DIFF_DK), 0.1),
        'diff_subln_w': 1.0 + nrm(ks[17], (DEPTH, DIFF_DV), 0.02),
        'w_branch': jnp.concatenate([nrm(ks[18], (DEPTH, GDN_V_DIM, D_MODEL), GDN_V_DIM ** -0.5),
                                     nrm(ks[19], (DEPTH, DIFF_V_DIM, D_MODEL), DIFF_V_DIM ** -0.5)], axis=1),
        'w_out': nrm(ks[20], (DEPTH, D_MODEL, D_MODEL), D_MODEL ** -0.5),
        'norm_ffn': 1.0 + nrm(ks[21], (DEPTH, D_MODEL), 0.02),
        'ffn_w_gate': nrm(ks[22], (DEPTH, D_MODEL, D_FF), D_MODEL ** -0.5),
        'ffn_w_up': nrm(ks[23], (DEPTH, D_MODEL, D_FF), D_MODEL ** -0.5),
        'ffn_conv_w': nrm(ks[24], (DEPTH, FFN_CONV, D_FF), FFN_CONV ** -0.5),
        'ffn_conv_b': nrm(ks[25], (DEPTH, D_FF), 0.01),
        'ffn_w_down': nrm(ks[26], (DEPTH, D_FF, D_MODEL), D_FF ** -0.5),
        'norm_final': 1.0 + nrm(ks[27], (D_MODEL,), 0.02),
    }


def reference(x_prompt, x_sample, cache_diff_k, cache_diff_v, state_gdn_conv, state_gdn_rec, state_ffn_conv,
              norm_mix, w_in, gdn_conv_w, gdn_a_log, gdn_dt_bias, gdn_norm_w,
              diff_lambda_q1, diff_lambda_k1, diff_lambda_q2, diff_lambda_k2, diff_subln_w,
              w_branch, w_out, norm_ffn, ffn_w_gate, ffn_w_up, ffn_conv_w, ffn_conv_b, ffn_w_down,
              norm_final):
    bp = x_prompt.shape[0]
    past = cache_diff_k.shape[2]
    dt = x_prompt.dtype
    hp, hs = x_prompt, x_sample
    p_out = [[], [], [], [], []]
    s_out = [[], [], [], [], []]
    for l in range(DEPTH):
        lam_init = 0.8 - 0.6 * math.exp(-0.3 * l)
        wl = (norm_mix[l], w_in[l], gdn_conv_w[l], gdn_a_log[l], gdn_dt_bias[l], gdn_norm_w[l],
              diff_lambda_q1[l], diff_lambda_k1[l], diff_lambda_q2[l], diff_lambda_k2[l], diff_subln_w[l],
              w_branch[l], w_out[l], norm_ffn[l], ffn_w_gate[l], ffn_w_up[l], ffn_conv_w[l],
              ffn_conv_b[l], ffn_w_down[l])
        hp, *sp = layer(hp, 0,
                        jnp.zeros((bp, 0, DIFF_H, 2 * DIFF_DK), dt),
                        jnp.zeros((bp, 0, DIFF_H, DIFF_DV), dt),
                        jnp.zeros((bp, GDN_CONV - 1, GDN_CONV_DIM), dt),
                        jnp.zeros((bp, GDN_HV, GDN_DK, GDN_DV), dt),
                        jnp.zeros((bp, FFN_CONV - 1, D_FF), dt),
                        lam_init, *wl)
        hs, *ss = layer(hs, past, cache_diff_k[l], cache_diff_v[l], state_gdn_conv[l], state_gdn_rec[l],
                        state_ffn_conv[l], lam_init, *wl)
        for i in range(5):
            p_out[i].append(sp[i])
            s_out[i].append(ss[i])
    y_prompt = rms_norm(hp, norm_final)
    y_sample = rms_norm(hs, norm_final)
    p_k, p_v, p_gdn_conv, p_gdn_rec, p_ffn_conv = [jnp.stack(a, axis=0) for a in p_out]
    s_k, s_v, s_gdn_conv, s_gdn_rec, s_ffn_conv = [jnp.stack(a, axis=0) for a in s_out]
    return (y_prompt, y_sample, p_k, p_v, p_gdn_conv, p_gdn_rec, p_ffn_conv,
            s_k, s_v, s_gdn_conv, s_gdn_rec, s_ffn_conv)
```

```python
import functools
import math

import jax
import jax.numpy as jnp
from jax import lax
from jax.experimental import pallas as pl
from jax.experimental.pallas import tpu as pltpu

F32 = jnp.float32
BF16 = jnp.bfloat16

CHUNK = 64
EPS = 1e-6
GDN_HK = 16
GDN_HV = 32
GDN_D = 128
GDN_CONV = 4
DIFF_H = 8
DIFF_DK = 128
DIFF_DV = 2 * DIFF_DK
ROPE_THETA = 500000.0
ROT_DIM = DIFF_DK // 4
FFN_CONV = 3
LANES = 128
SUBLANES = 8
NEG_BIG = -1e30
VMEM_LIMIT = 56 * 1024 * 1024


def _cparams(sem):
    return pltpu.CompilerParams(dimension_semantics=sem, vmem_limit_bytes=VMEM_LIMIT)


def _sigmoid(x):
    return 1.0 / (1.0 + jnp.exp(-x))


def _silu(x):
    return x * _sigmoid(x)


def _bdot(a, b):
    return jnp.dot(a.astype(BF16), b.astype(BF16), preferred_element_type=F32)


def _bdot_nt(a, b):
    return lax.dot_general(a.astype(BF16), b.astype(BF16), (((1,), (1,)), ((), ())),
                           preferred_element_type=F32)


def _shift_rows(x, prev8, s):
    n = x.shape[0]
    xr = pltpu.roll(x, s, axis=0)
    pr = pltpu.roll(prev8, s, axis=0)
    row = lax.broadcasted_iota(jnp.int32, pr.shape, 0)
    head = jnp.where(row < s, pr, xr[:SUBLANES])
    if n == SUBLANES:
        return head
    return jnp.concatenate([head, xr[SUBLANES:]], axis=0)


def _inproj_kernel(x_ref, nw_ref, w_ref, wg_ref, o_ref, og_ref, xn_ref):
    @pl.when(pl.program_id(1) == 0)
    def _():
        x = x_ref[...]
        ms = jnp.mean(x * x, axis=-1, keepdims=True)
        xn = ((x * lax.rsqrt(ms + EPS)) * nw_ref[...]).astype(BF16)
        xn_ref[...] = xn
        og_ref[...] = jnp.dot(xn, wg_ref[...], preferred_element_type=F32)

    o_ref[...] = jnp.dot(xn_ref[...], w_ref[...], preferred_element_type=F32)


def _inproj(x, norm_w, w_main, w_gates, tm, tn):
    m, d = x.shape
    n = w_main.shape[1]
    return pl.pallas_call(
        _inproj_kernel,
        grid=(m // tm, n // tn),
        in_specs=[
            pl.BlockSpec((tm, d), lambda i, j: (i, 0)),
            pl.BlockSpec((1, d), lambda i, j: (0, 0)),
            pl.BlockSpec((d, tn), lambda i, j: (0, j)),
            pl.BlockSpec((d, LANES), lambda i, j: (0, 0)),
        ],
        out_specs=[
            pl.BlockSpec((tm, tn), lambda i, j: (i, j)),
            pl.BlockSpec((tm, LANES), lambda i, j: (i, 0)),
        ],
        out_shape=[jax.ShapeDtypeStruct((m, n), F32), jax.ShapeDtypeStruct((m, LANES), F32)],
        scratch_shapes=[pltpu.VMEM((tm, d), BF16)],
        compiler_params=_cparams(("arbitrary", "arbitrary")),
        name="in_proj",
    )(x, norm_w, w_main, w_gates)


def _rope_kernel(dq_ref, dk_ref, dv_ref, c_ref, s1_ref, s2_ref,
                 q_ref, kf_ref, kb_ref, vb_ref, *, scale):
    c = c_ref[...]
    s1 = s1_ref[...]
    s2 = s2_ref[...]
    half = ROT_DIM // 2
    for g in range(dq_ref.shape[1] // LANES):
        sl = slice(g * LANES, (g + 1) * LANES)
        xq = dq_ref[:, sl]
        rq = xq * c + pltpu.roll(xq, LANES - half, axis=1) * s1 + pltpu.roll(xq, half, axis=1) * s2
        q_ref[:, sl] = (rq * scale).astype(BF16)
        xk = dk_ref[:, sl]
        rk = xk * c + pltpu.roll(xk, LANES - half, axis=1) * s1 + pltpu.roll(xk, half, axis=1) * s2
        kf_ref[:, sl] = rk
        kb_ref[:, sl] = rk.astype(BF16)
    vb_ref[...] = dv_ref[...].astype(BF16)


def _rope_tables(pos):
    half = ROT_DIM // 2
    inv = jnp.exp(-math.log(ROPE_THETA) * jnp.arange(0, ROT_DIM, 2, dtype=F32) / ROT_DIM)
    ang = pos.astype(F32)[:, None] * inv[None, :]
    cos, sin = jnp.cos(ang), jnp.sin(ang)
    n = pos.shape[0]
    ones = jnp.ones((n, LANES - ROT_DIM), F32)
    zeros = jnp.zeros((n, LANES - half), F32)
    c = jnp.concatenate([cos, cos, ones], axis=1)
    s1 = jnp.concatenate([-sin, zeros], axis=1)
    s2 = jnp.concatenate([jnp.zeros((n, half), F32), sin, jnp.zeros((n, LANES - ROT_DIM), F32)], axis=1)
    return c, s1, s2


def _rope(proj, tables, tm, n_tab_blocks, off_q, off_k, off_v, width):
    m = proj.shape[0]
    wb = width
    row = lambda i: (i, 0)
    tab = lambda i: (i % n_tab_blocks, 0)
    outs = pl.pallas_call(
        functools.partial(_rope_kernel, scale=DIFF_DK ** -0.5),
        grid=(m // tm,),
        in_specs=[
            pl.BlockSpec((tm, wb), lambda i: (i, off_q // wb)),
            pl.BlockSpec((tm, wb), lambda i: (i, off_k // wb)),
            pl.BlockSpec((tm, wb), lambda i: (i, off_v // wb)),
            pl.BlockSpec((tm, LANES), tab),
            pl.BlockSpec((tm, LANES), tab),
            pl.BlockSpec((tm, LANES), tab),
        ],
        out_specs=[pl.BlockSpec((tm, wb), row)] * 4,
        out_shape=[jax.ShapeDtypeStruct((m, wb), BF16), jax.ShapeDtypeStruct((m, wb), F32),
                   jax.ShapeDtypeStruct((m, wb), BF16), jax.ShapeDtypeStruct((m, wb), BF16)],
        compiler_params=_cparams(("arbitrary",)),
        name="rope",
    )(proj, proj, proj, *tables)
    return outs


def _split2(x):
    hi = x.astype(BF16)
    lo = (x - hi.astype(F32)).astype(BF16)
    return hi, lo


def _gdn_kernel(q_ref, k_ref, v_ref, z_ref, g_ref, hq_ref, hk_ref, hv_ref,
                cwq_ref, cwk_ref, cwv_ref, alog_ref, dtb_ref, nw_ref, s0_ref,
                o_ref, sfin_ref, s_scr, pq_scr, pk_scr, pv_scr, *, chunk):
    h = pl.program_id(1)
    t = pl.program_id(2)
    nt = pl.num_programs(2)
    tt = q_ref.shape[0]
    d = GDN_D
    c = chunk

    @pl.when(t == 0)
    def _():
        s_scr[...] = s0_ref[0]
        pq_scr[...] = hq_ref[0]
        pk_scr[...] = hk_ref[0]
        pv_scr[...] = hv_ref[0]

    def conv_silu(x, prev, w_ref):
        acc = x * w_ref[GDN_CONV - 1:GDN_CONV, :]
        for s in range(1, GDN_CONV):
            acc = acc + _shift_rows(x, prev, s) * w_ref[GDN_CONV - 1 - s:GDN_CONV - s, :]
        return _silu(acc)

    xq = q_ref[...]
    xk = k_ref[...]
    xv = v_ref[...]
    q = conv_silu(xq, pq_scr[...], cwq_ref)
    k = conv_silu(xk, pk_scr[...], cwk_ref)
    v = conv_silu(xv, pv_scr[...], cwv_ref)
    pq_scr[...] = xq[tt - SUBLANES:]
    pk_scr[...] = xk[tt - SUBLANES:]
    pv_scr[...] = xv[tt - SUBLANES:]

    q = q * lax.rsqrt(jnp.sum(q * q, axis=-1, keepdims=True) + EPS) * (d ** -0.5)
    k = k * lax.rsqrt(jnp.sum(k * k, axis=-1, keepdims=True) + EPS)

    gates = g_ref[...]
    xg = gates + dtb_ref[...]
    softplus = jnp.maximum(xg, 0.0) + jnp.log(1.0 + jnp.exp(-jnp.abs(xg)))
    g_all = -jnp.exp(alog_ref[...]) * softplus
    beta_all = _sigmoid(gates)
    lane = lax.broadcasted_iota(jnp.int32, gates.shape, 1)

    ri = lax.broadcasted_iota(jnp.int32, (c, c), 0)
    ci = lax.broadcasted_iota(jnp.int32, (c, c), 1)
    causal = ri >= ci
    strict = ri > ci
    tril_b = causal.astype(BF16)
    eye = (ri == ci).astype(F32)
    rm = lax.broadcasted_iota(jnp.int32, (c, 2 * LANES), 0)
    cm = lax.broadcasted_iota(jnp.int32, (c, 2 * LANES), 1)
    cum_mask = jnp.where(((cm < c) & (rm > cm)) | (cm >= LANES), 1.0, 0.0)

    z = z_ref[...]
    nw = nw_ref[...]
    n_steps = int(math.log2(c)) - 1

    for e in range(2):
        hv = 2 * h + e
        g_col = jnp.sum(jnp.where(lane == hv, g_all, 0.0), axis=-1, keepdims=True)
        b_col = jnp.sum(jnp.where(lane == GDN_HV + hv, beta_all, 0.0), axis=-1, keepdims=True)
        ve = v[:, e * d:(e + 1) * d]
        s = s_scr[e]
        for n in range(tt // c):
            rs = slice(n * c, (n + 1) * c)
            qc, kc, vc = q[rs], k[rs], ve[rs]
            gc = jnp.broadcast_to(g_col[rs], (c, 2 * LANES)) * cum_mask
            g_hi, g_lo = _split2(gc)
            gd = (jnp.dot(tril_b, g_hi, preferred_element_type=F32)
                  + jnp.dot(tril_b, g_lo, preferred_element_type=F32))
            gdiff = gd[:, :c]
            gb = gd[:, LANES:]
            decay = jnp.where(causal, jnp.exp(gdiff), 0.0)
            exp_g = jnp.exp(gb)
            g_last = gb[c - 1:c, :]
            k_dec = jnp.exp(g_last - gb)
            exp_gl = jnp.exp(g_last)
            bb = jnp.broadcast_to(b_col[rs], (c, LANES))

            kk = _bdot_nt(kc, kc)
            qk = _bdot_nt(qc, kc)
            a = jnp.where(strict, -(kk * bb[:, :c] * decay), 0.0)
            tinv = eye + a
            for _ in range(n_steps):
                a = _bdot(a, a)
                tinv = tinv + _bdot(tinv, a)
            rhs = jnp.concatenate([vc * bb, kc * bb * exp_g], axis=1)
            uw = _bdot(tinv, rhs)
            u, w = uw[:, :d], uw[:, d:]
            intra = jnp.where(causal, qk * decay, 0.0)
            qg = qc * exp_g
            kd = kc * k_dec

            ws = _bdot(jnp.concatenate([w, qg], axis=0), s)
            v_new = u - ws[:c]
            o = ws[c:] + _bdot(intra, v_new)
            s = s * exp_gl + _bdot(kd.T, v_new)

            o = o * lax.rsqrt(jnp.mean(o * o, axis=-1, keepdims=True) + EPS) * nw
            o = o * _silu(z[rs, e * d:(e + 1) * d])
            o_ref[rs, e * d:(e + 1) * d] = o.astype(o_ref.dtype)
        s_scr[e] = s

    @pl.when(t == nt - 1)
    def _():
        sfin_ref[0] = s_scr[...]


def _gdn(proj, gates, hist8, conv_w, alog, dtb, nw, s0, nb, t_len, tt, chunk):
    m = proj.shape[0]
    nt = t_len // tt
    hk = GDN_HK
    d = GDN_D
    row = lambda b, h, t: b * nt + t
    ya, sfin = pl.pallas_call(
        functools.partial(_gdn_kernel, chunk=chunk),
        grid=(nb, hk, nt),
        in_specs=[
            pl.BlockSpec((tt, d), lambda b, h, t: (row(b, h, t), h)),
            pl.BlockSpec((tt, d), lambda b, h, t: (row(b, h, t), hk + h)),
            pl.BlockSpec((tt, 2 * d), lambda b, h, t: (row(b, h, t), hk + h)),
            pl.BlockSpec((tt, 2 * d), lambda b, h, t: (row(b, h, t), 2 * hk + h)),
            pl.BlockSpec((tt, LANES), lambda b, h, t: (row(b, h, t), 0)),
            pl.BlockSpec((1, SUBLANES, d), lambda b, h, t: (b, 0, h)),
            pl.BlockSpec((1, SUBLANES, d), lambda b, h, t: (b, 0, hk + h)),
            pl.BlockSpec((1, SUBLANES, 2 * d), lambda b, h, t: (b, 0, hk + h)),
            pl.BlockSpec((GDN_CONV, d), lambda b, h, t: (0, h)),
            pl.BlockSpec((GDN_CONV, d), lambda b, h, t: (0, hk + h)),
            pl.BlockSpec((GDN_CONV, 2 * d), lambda b, h, t: (0, hk + h)),
            pl.BlockSpec((1, LANES), lambda b, h, t: (0, 0)),
            pl.BlockSpec((1, LANES), lambda b, h, t: (0, 0)),
            pl.BlockSpec((1, d), lambda b, h, t: (0, 0)),
            pl.BlockSpec((1, 2, d, d), lambda b, h, t: (b, h, 0, 0)),
        ],
        out_specs=[
            pl.BlockSpec((tt, 2 * d), lambda b, h, t: (row(b, h, t), h)),
            pl.BlockSpec((1, 2, d, d), lambda b, h, t: (b, h, 0, 0)),
        ],
        out_shape=[jax.ShapeDtypeStruct((m, GDN_HV * d), BF16),
                   jax.ShapeDtypeStruct((nb, GDN_HV, d, d), F32)],
        scratch_shapes=[pltpu.VMEM((2, d, d), F32), pltpu.VMEM((SUBLANES, d), F32),
                        pltpu.VMEM((SUBLANES, d), F32), pltpu.VMEM((SUBLANES, 2 * d), F32)],
        compiler_params=_cparams(("arbitrary", "arbitrary", "arbitrary")),
        name="gdn",
    )(proj, proj, proj, proj, gates, hist8, hist8, hist8, conv_w, conv_w, conv_w,
      alog, dtb, nw, s0)
    return ya, sfin


def _attn_kernel(q_ref, k_ref, v_ref, lam_ref, sw_ref, o_ref, acc_ref, m_ref, l_ref,
                 *, tk, q_off, lam_init):
    i = pl.program_id(2)
    tq = q_ref.shape[0]
    s_len = k_ref.shape[0]
    dk = DIFF_DK
    shift = int(math.log2(CHUNK))
    q0 = q_off + i * tq
    lo_vis = jnp.minimum(((q0 >> shift) + 1) * CHUNK, s_len)
    hi_vis = jnp.minimum((((q0 + tq - 1) >> shift) + 1) * CHUNK, s_len)
    n_full = lo_vis // tk
    n_tot = (hi_vis + tk - 1) // tk

    m_ref[...] = jnp.full(m_ref.shape, NEG_BIG, F32)
    l_ref[...] = jnp.zeros(l_ref.shape, F32)
    acc_ref[...] = jnp.zeros(acc_ref.shape, F32)
    q = q_ref[...]

    def step(j, masked):
        k0 = pl.multiple_of(j * tk, tk)
        ks = k_ref[pl.ds(k0, tk), :]
        vs = v_ref[pl.ds(k0, tk), :]
        for c in range(2):
            s = lax.dot_general(q[:, c * dk:(c + 1) * dk], ks[:, c * dk:(c + 1) * dk],
                                (((1,), (1,)), ((), ())), preferred_element_type=F32)
            if masked:
                qpos = q0 + lax.broadcasted_iota(jnp.int32, s.shape, 0)
                kpos = k0 + lax.broadcasted_iota(jnp.int32, s.shape, 1)
                s = jnp.where((kpos >> shift) <= (qpos >> shift), s, NEG_BIG)
            m_prev = m_ref[c]
            m_new = jnp.maximum(m_prev, jnp.max(s, axis=-1, keepdims=True))
            alpha = jnp.exp(m_prev - m_new)
            p = jnp.exp(s - m_new)
            l_ref[c] = alpha * l_ref[c] + jnp.sum(p, axis=-1, keepdims=True)
            acc_ref[c] = alpha * acc_ref[c] + jnp.dot(p.astype(BF16), vs, preferred_element_type=F32)
            m_ref[c] = m_new

    def body_full(j, carry):
        step(j, False)
        return carry

    def body_masked(j, carry):
        step(j, True)
        return carry

    lax.fori_loop(0, n_full, body_full, 0)
    lax.fori_loop(n_full, n_tot, body_masked, 0)

    lam = (jnp.exp(jnp.sum(lam_ref[0:1, :] * lam_ref[1:2, :], axis=-1, keepdims=True))
           - jnp.exp(jnp.sum(lam_ref[2:3, :] * lam_ref[3:4, :], axis=-1, keepdims=True)) + lam_init)
    o = acc_ref[0] / l_ref[0] - lam * (acc_ref[1] / l_ref[1])
    o = (o * lax.rsqrt(jnp.mean(o * o, axis=-1, keepdims=True) + EPS)) * sw_ref[...]
    o_ref[...] = (o * (1.0 - lam_init)).astype(o_ref.dtype)


def _attn(q, k, v, lam4, subln_w, nb, t_len, s_len, tq, tk, q_off, lam_init):
    m = q.shape[0]
    nq = t_len // tq
    dv = DIFF_DV
    return pl.pallas_call(
        functools.partial(_attn_kernel, tk=tk, q_off=q_off, lam_init=lam_init),
        grid=(nb, DIFF_H, nq),
        in_specs=[
            pl.BlockSpec((tq, dv), lambda b, h, i: (b * nq + i, h)),
            pl.BlockSpec((s_len, dv), lambda b, h, i: (b, h)),
            pl.BlockSpec((s_len, dv), lambda b, h, i: (b, h)),
            pl.BlockSpec((4, DIFF_DK), lambda b, h, i: (0, 0)),
            pl.BlockSpec((1, dv), lambda b, h, i: (0, 0)),
        ],
        out_specs=pl.BlockSpec((tq, dv), lambda b, h, i: (b * nq + i, h)),
        out_shape=jax.ShapeDtypeStruct((m, DIFF_H * dv), BF16),
        scratch_shapes=[pltpu.VMEM((2, tq, dv), F32), pltpu.VMEM((2, tq, 1), F32),
                        pltpu.VMEM((2, tq, 1), F32)],
        compiler_params=_cparams(("arbitrary", "arbitrary", "arbitrary")),
        name="diff_attn",
    )(q, k, v, lam4, subln_w)


def _merge_kernel(ya_ref, yb_ref, ga_ref, gb_ref, wa_ref, wb_ref, o_ref):
    pa = jnp.dot(ya_ref[...], wa_ref[...], preferred_element_type=F32)
    pb = jnp.dot(yb_ref[...], wb_ref[...], preferred_element_type=F32)
    o_ref[...] = (_sigmoid(ga_ref[...]) * pa + _sigmoid(gb_ref[...]) * pb).astype(o_ref.dtype)


def _merge(ya, yb, proj, wa, wb, off_ga, off_gb, tm, tn):
    m = ya.shape[0]
    n = wa.shape[1]
    return pl.pallas_call(
        _merge_kernel,
        grid=(m // tm, n // tn),
        in_specs=[
            pl.BlockSpec((tm, ya.shape[1]), lambda i, j: (i, 0)),
            pl.BlockSpec((tm, yb.shape[1]), lambda i, j: (i, 0)),
            pl.BlockSpec((tm, tn), lambda i, j: (i, off_ga // tn + j)),
            pl.BlockSpec((tm, tn), lambda i, j: (i, off_gb // tn + j)),
            pl.BlockSpec((wa.shape[0], tn), lambda i, j: (0, j)),
            pl.BlockSpec((wb.shape[0], tn), lambda i, j: (0, j)),
        ],
        out_specs=pl.BlockSpec((tm, tn), lambda i, j: (i, j)),
        out_shape=jax.ShapeDtypeStruct((m, n), BF16),
        compiler_params=_cparams(("arbitrary", "arbitrary")),
        name="merge",
    )(ya, yb, proj, proj, wa, wb)


def _outproj_kernel(mg_ref, x_ref, w_ref, o_ref):
    o_ref[...] = x_ref[...] + jnp.dot(mg_ref[...], w_ref[...], preferred_element_type=F32)


def _outproj(merged, x, w, tm, tn):
    m, d = x.shape
    return pl.pallas_call(
        _outproj_kernel,
        grid=(m // tm, d // tn),
        in_specs=[
            pl.BlockSpec((tm, merged.shape[1]), lambda i, j: (i, 0)),
            pl.BlockSpec((tm, tn), lambda i, j: (i, j)),
            pl.BlockSpec((w.shape[0], tn), lambda i, j: (0, j)),
        ],
        out_specs=pl.BlockSpec((tm, tn), lambda i, j: (i, j)),
        out_shape=jax.ShapeDtypeStruct((m, d), F32),
        compiler_params=_cparams(("arbitrary", "arbitrary")),
        name="out_proj",
    )(merged, x, w)


def _ffn_kernel(h_ref, nw_ref, wg_ref, wu_ref, cw_ref, cb_ref, wd_ref, hist_ref, nf_ref,
                y_ref, tail_ref, hn_scr, acc_scr, carry_scr, *, n_seq, tiles_per_seq,
                final_norm):
    i = pl.program_id(0)
    j = pl.program_id(1)
    nj = pl.num_programs(1)
    tm = h_ref.shape[0]
    tf = wg_ref.shape[1]
    ts = tm // n_seq

    @pl.when(j == 0)
    def _():
        hh = h_ref[...]
        ms = jnp.mean(hh * hh, axis=-1, keepdims=True)
        hn_scr[...] = ((hh * lax.rsqrt(ms + EPS)) * nw_ref[...]).astype(BF16)
        acc_scr[...] = jnp.zeros(acc_scr.shape, F32)

    hn = hn_scr[...]
    g = jnp.dot(hn, wg_ref[...], preferred_element_type=F32)
    up = jnp.dot(hn, wu_ref[...], preferred_element_type=F32)

    @pl.when((i % tiles_per_seq) == 0)
    def _():
        carry_scr[j] = hist_ref[...]

    w0 = cw_ref[0:1, :]
    w1 = cw_ref[1:2, :]
    w2 = cw_ref[2:3, :]
    cb = cb_ref[...]
    outs = []
    for s in range(n_seq):
        gs = g[s * ts:(s + 1) * ts]
        prev = carry_scr[j, s]
        conv = gs * w2 + _shift_rows(gs, prev, 1) * w1 + _shift_rows(gs, prev, 2) * w0 + cb
        outs.append(_silu(conv) * up[s * ts:(s + 1) * ts])
        tail = gs[ts - SUBLANES:]
        carry_scr[j, s] = tail
        tail_ref[s] = tail
    act = outs[0] if n_seq == 1 else jnp.concatenate(outs, axis=0)
    acc_scr[...] += jnp.dot(act.astype(BF16), wd_ref[...], preferred_element_type=F32)

    @pl.when(j == nj - 1)
    def _():
        o = h_ref[...] + acc_scr[...]
        if final_norm:
            ms = jnp.mean(o * o, axis=-1, keepdims=True)
            o = (o * lax.rsqrt(ms + EPS)) * nf_ref[...]
        y_ref[...] = o


def _ffn(h, norm_w, wg, wu, conv_w, conv_b, wd, hist8, norm_final, final_norm, t_len, tm, tf):
    m, d = h.shape
    f = wg.shape[1]
    nj = f // tf
    if tm >= t_len:
        n_seq, tiles_per_seq = tm // t_len, 1
    else:
        n_seq, tiles_per_seq = 1, t_len // tm
    n_tiles = m // tm
    seq_blk = lambda i, j: ((i // tiles_per_seq), 0, j)
    y, tails = pl.pallas_call(
        functools.partial(_ffn_kernel, n_seq=n_seq, tiles_per_seq=tiles_per_seq, final_norm=final_norm),
        grid=(n_tiles, nj),
        in_specs=[
            pl.BlockSpec((tm, d), lambda i, j: (i, 0)),
            pl.BlockSpec((1, d), lambda i, j: (0, 0)),
            pl.BlockSpec((d, tf), lambda i, j: (0, j)),
            pl.BlockSpec((d, tf), lambda i, j: (0, j)),
            pl.BlockSpec((FFN_CONV, tf), lambda i, j: (0, j)),
            pl.BlockSpec((1, tf), lambda i, j: (0, j)),
            pl.BlockSpec((tf, d), lambda i, j: (j, 0)),
            pl.BlockSpec((n_seq, SUBLANES, tf), seq_blk),
            pl.BlockSpec((1, d), lambda i, j: (0, 0)),
        ],
        out_specs=[
            pl.BlockSpec((tm, d), lambda i, j: (i, 0)),
            pl.BlockSpec((n_seq, SUBLANES, tf), lambda i, j: (i, 0, j)),
        ],
        out_shape=[jax.ShapeDtypeStruct((m, d), F32),
                   jax.ShapeDtypeStruct((n_tiles * n_seq, SUBLANES, f), F32)],
        scratch_shapes=[pltpu.VMEM((tm, d), BF16), pltpu.VMEM((tm, d), F32),
                        pltpu.VMEM((nj, n_seq, SUBLANES, tf), F32)],
        compiler_params=_cparams(("arbitrary", "arbitrary")),
        name="conv_ffn",
    )(h, norm_w, wg, wu, conv_w, conv_b, wd, hist8, norm_final)
    return y, tails


def _pad_hist(hist):
    b, r, c = hist.shape
    return jnp.concatenate([jnp.zeros((b, SUBLANES - r, c), hist.dtype), hist], axis=1)


def _pick(n, cands):
    for c in cands:
        if n % c == 0:
            return c
    return n


def _layer(x, pos_offset, k_hist, v_hist, gdn_conv_hist, gdn_s0, ffn_hist, lam_init, wts, norm_final,
           final_norm):
    nb, t_len, d_model = x.shape
    m = nb * t_len
    past = k_hist.shape[1]
    x2 = x.reshape(m, d_model)
    qk_dim = GDN_HK * GDN_D
    v_dim = GDN_HV * GDN_D
    dqk = DIFF_H * 2 * DIFF_DK
    off = {"gq": 0, "gk": qk_dim, "gv": 2 * qk_dim, "gz": 2 * qk_dim + v_dim}
    off["dq"] = off["gz"] + v_dim
    off["dk"] = off["dq"] + dqk
    off["dv"] = off["dk"] + dqk
    off["ga"] = off["dv"] + DIFF_H * DIFF_DV
    off["gb"] = off["ga"] + d_model

    tm = _pick(m, (1024, 512, 256))
    tn = _pick(wts["w_main"].shape[1], (1024, 512, 256, 128))
    proj, gates = _inproj(x2, wts["norm_mix"], wts["w_main"], wts["w_gates"], tm, tn)

    tmr = _pick(m, (512, 256))
    pos = pos_offset + jnp.arange(t_len, dtype=jnp.int32)
    tables = _rope_tables(pos)
    if t_len < tmr:
        tables = [jnp.tile(tb, (tmr // t_len, 1)) for tb in tables]
    n_tab = max(t_len // tmr, 1)
    q_r, k_rows, k_b, v_b = _rope(proj, tables, tmr, n_tab, off["dq"], off["dk"], off["dv"], dqk)
    v_rows = proj[:, off["dv"]:off["dv"] + DIFF_H * DIFF_DV]

    chunk = CHUNK if t_len % CHUNK == 0 else t_len
    tt = _pick(t_len, (256,)) if t_len % CHUNK == 0 else t_len
    ya, gdn_s = _gdn(proj, gates, _pad_hist(gdn_conv_hist), wts["gdn_conv_w"], wts["alog"], wts["dtb"],
                     wts["gdn_norm_w"], gdn_s0, nb, t_len, tt, chunk)
    qkv_tail = proj.reshape(nb, t_len, -1)[:, t_len - (GDN_CONV - 1):, :off["gz"]]
    if t_len >= GDN_CONV - 1:
        gdn_conv_new = qkv_tail
    else:
        gdn_conv_new = jnp.concatenate([gdn_conv_hist, proj.reshape(nb, t_len, -1)[:, :, :off["gz"]]],
                                       axis=1)[:, -(GDN_CONV - 1):]

    s_len = past + t_len
    if past:
        k_all = jnp.concatenate([k_hist.reshape(nb, past, dqk).astype(BF16), k_b.reshape(nb, t_len, dqk)],
                                axis=1).reshape(nb * s_len, dqk)
        v_all = jnp.concatenate([v_hist.reshape(nb, past, dqk).astype(BF16), v_b.reshape(nb, t_len, dqk)],
                                axis=1).reshape(nb * s_len, dqk)
    else:
        k_all, v_all = k_b, v_b
    tq = _pick(t_len, (256,))
    tk = 256 if s_len % 256 == 0 else s_len
    yb = _attn(q_r, k_all, v_all, wts["lam4"], wts["subln_w"], nb, t_len, s_len, tq, tk,
               pos_offset, lam_init)

    tmm = _pick(m, (512, 256))
    tnm = _pick(d_model, (512, 256, 128))
    merged = _merge(ya, yb, proj, wts["w_br_a"], wts["w_br_b"], off["ga"], off["gb"], tmm, tnm)
    h = _outproj(merged, x2, wts["w_out"], tmm, tnm)
    tmf = _pick(m, (512, 256))
    y, tails = _ffn(h, wts["norm_ffn"], wts["w_gate"], wts["w_up"], wts["ffn_conv_w"], wts["ffn_conv_b"],
                    wts["w_down"], _pad_hist(ffn_hist), norm_final, final_norm, t_len, tmf, 512)
    d_ff = tails.shape[-1]
    if tmf >= t_len:
        seq_tails = tails
    else:
        seq_tails = tails.reshape(nb, t_len // tmf, SUBLANES, d_ff)[:, -1]
    ffn_new = seq_tails[:, SUBLANES - (FFN_CONV - 1):]

    return (y.reshape(nb, t_len, d_model),
            k_rows.reshape(nb, t_len, DIFF_H, 2 * DIFF_DK),
            v_rows.reshape(nb, t_len, DIFF_H, DIFF_DV),
            gdn_conv_new, gdn_s, ffn_new)


def _prep_weights(l, norm_mix, w_in, gdn_conv_w, gdn_a_log, gdn_dt_bias, gdn_norm_w,
                  lq1, lk1, lq2, lk2, subln_w, w_branch, w_out, norm_ffn,
                  w_gate, w_up, ffn_conv_w, ffn_conv_b, w_down):
    qk_dim = GDN_HK * GDN_D
    v_dim = GDN_HV * GDN_D
    g0 = 2 * qk_dim + 2 * v_dim
    g1 = g0 + 2 * GDN_HV
    wi = w_in[l]
    d_model = wi.shape[0]
    w_main = jnp.concatenate([wi[:, :g0], wi[:, g1:]], axis=1).astype(BF16)
    w_gates = jnp.concatenate([wi[:, g0:g1], jnp.zeros((d_model, LANES - 2 * GDN_HV), wi.dtype)],
                              axis=1).astype(BF16)
    pad = jnp.zeros((LANES - GDN_HV,), F32)
    return {
        "norm_mix": norm_mix[l].reshape(1, -1),
        "w_main": w_main,
        "w_gates": w_gates,
        "gdn_conv_w": gdn_conv_w[l],
        "alog": jnp.concatenate([gdn_a_log[l].astype(F32), pad]).reshape(1, LANES),
        "dtb": jnp.concatenate([gdn_dt_bias[l].astype(F32), pad]).reshape(1, LANES),
        "gdn_norm_w": gdn_norm_w[l].reshape(1, -1),
        "lam4": jnp.stack([lq1[l], lk1[l], lq2[l], lk2[l]], axis=0).astype(F32),
        "subln_w": subln_w[l].reshape(1, -1),
        "w_br_a": w_branch[l][:v_dim].astype(BF16),
        "w_br_b": w_branch[l][v_dim:].astype(BF16),
        "w_out": w_out[l].astype(BF16),
        "norm_ffn": norm_ffn[l].reshape(1, -1),
        "w_gate": w_gate[l].astype(BF16),
        "w_up": w_up[l].astype(BF16),
        "ffn_conv_w": ffn_conv_w[l],
        "ffn_conv_b": ffn_conv_b[l].reshape(1, -1),
        "w_down": w_down[l].astype(BF16),
    }


def kernel(x_prompt, x_sample, cache_diff_k, cache_diff_v, state_gdn_conv, state_gdn_rec, state_ffn_conv,
           norm_mix, w_in, gdn_conv_w, gdn_a_log, gdn_dt_bias, gdn_norm_w,
           diff_lambda_q1, diff_lambda_k1, diff_lambda_q2, diff_lambda_k2, diff_subln_w,
           w_branch, w_out, norm_ffn, ffn_w_gate, ffn_w_up, ffn_conv_w, ffn_conv_b, ffn_w_down,
           norm_final):
    depth = w_in.shape[0]
    bp = x_prompt.shape[0]
    past = cache_diff_k.shape[2]
    dt = x_prompt.dtype
    d_ff = ffn_w_gate.shape[-1]
    conv_dim = gdn_conv_w.shape[-1]
    nf = norm_final.reshape(1, -1)
    hp, hs = x_prompt, x_sample
    p_out = [[], [], [], [], []]
    s_out = [[], [], [], [], []]
    for l in range(depth):
        lam_init = 0.8 - 0.6 * math.exp(-0.3 * l)
        wts = _prep_weights(l, norm_mix, w_in, gdn_conv_w, gdn_a_log, gdn_dt_bias, gdn_norm_w,
                            diff_lambda_q1, diff_lambda_k1, diff_lambda_q2, diff_lambda_k2, diff_subln_w,
                            w_branch, w_out, norm_ffn, ffn_w_gate, ffn_w_up, ffn_conv_w, ffn_conv_b,
                            ffn_w_down)
        last = l == depth - 1
        hp, *sp = _layer(hp, 0,
                         jnp.zeros((bp, 0, DIFF_H, 2 * DIFF_DK), dt),
                         jnp.zeros((bp, 0, DIFF_H, DIFF_DV), dt),
                         jnp.zeros((bp, GDN_CONV - 1, conv_dim), dt),
                         jnp.zeros((bp, GDN_HV, GDN_D, GDN_D), dt),
                         jnp.zeros((bp, FFN_CONV - 1, d_ff), dt),
                         lam_init, wts, nf, last)
        hs, *ss = _layer(hs, past, cache_diff_k[l], cache_diff_v[l], state_gdn_conv[l], state_gdn_rec[l],
                         state_ffn_conv[l], lam_init, wts, nf, last)
        for i in range(5):
            p_out[i].append(sp[i])
            s_out[i].append(ss[i])
    p_k, p_v, p_gdn_conv, p_gdn_rec, p_ffn_conv = [jnp.stack(a, axis=0) for a in p_out]
    s_k, s_v, s_gdn_conv, s_gdn_rec, s_ffn_conv = [jnp.stack(a, axis=0) for a in s_out]
    return (hp, hs, p_k, p_v, p_gdn_conv, p_gdn_rec, p_ffn_conv,
            s_k, s_v, s_gdn_conv, s_gdn_rec, s_ffn_conv)
```

```python
import functools
import math

import jax
import jax.numpy as jnp
from jax import lax
from jax.experimental import pallas as pl
from jax.experimental.pallas import tpu as pltpu

F32 = jnp.float32
BF16 = jnp.bfloat16

CHUNK = 64
EPS = 1e-6
GDN_HK = 16
GDN_HV = 32
GDN_D = 128
GDN_CONV = 4
DIFF_H = 8
DIFF_DK = 128
DIFF_DV = 2 * DIFF_DK
ROPE_THETA = 500000.0
ROT_DIM = DIFF_DK // 4
FFN_CONV = 3
LANES = 128
SUBLANES = 8
NEG_BIG = -1e30
VMEM_LIMIT = 56 * 1024 * 1024


def _cparams(sem):
    return pltpu.CompilerParams(dimension_semantics=sem, vmem_limit_bytes=VMEM_LIMIT)


def _sigmoid(x):
    return 1.0 / (1.0 + jnp.exp(-x))


def _silu(x):
    return x * _sigmoid(x)


def _bdot(a, b):
    return jnp.dot(a.astype(BF16), b.astype(BF16), preferred_element_type=F32)


def _bdot_nt(a, b):
    return lax.dot_general(a.astype(BF16), b.astype(BF16), (((1,), (1,)), ((), ())),
                           preferred_element_type=F32)


def _shift_rows(x, prev8, s):
    n = x.shape[0]
    xr = pltpu.roll(x, s, axis=0)
    pr = pltpu.roll(prev8, s, axis=0)
    row = lax.broadcasted_iota(jnp.int32, pr.shape, 0)
    head = jnp.where(row < s, pr, xr[:SUBLANES])
    if n == SUBLANES:
        return head
    return jnp.concatenate([head, xr[SUBLANES:]], axis=0)


def _inproj_kernel(x_ref, nw_ref, w_ref, wg_ref, o_ref, og_ref, xn_ref):
    @pl.when(pl.program_id(1) == 0)
    def _():
        x = x_ref[...]
        ms = jnp.mean(x * x, axis=-1, keepdims=True)
        xn = ((x * lax.rsqrt(ms + EPS)) * nw_ref[...]).astype(BF16)
        xn_ref[...] = xn
        og_ref[...] = jnp.dot(xn, wg_ref[...], preferred_element_type=F32)

    o_ref[...] = jnp.dot(xn_ref[...], w_ref[...], preferred_element_type=F32)


def _inproj(x, norm_w, w_main, w_gates, tm, tn):
    m, d = x.shape
    n = w_main.shape[1]
    return pl.pallas_call(
        _inproj_kernel,
        grid=(m // tm, n // tn),
        in_specs=[
            pl.BlockSpec((tm, d), lambda i, j: (i, 0)),
            pl.BlockSpec((1, d), lambda i, j: (0, 0)),
            pl.BlockSpec((d, tn), lambda i, j: (0, j)),
            pl.BlockSpec((d, LANES), lambda i, j: (0, 0)),
        ],
        out_specs=[
            pl.BlockSpec((tm, tn), lambda i, j: (i, j)),
            pl.BlockSpec((tm, LANES), lambda i, j: (i, 0)),
        ],
        out_shape=[jax.ShapeDtypeStruct((m, n), F32), jax.ShapeDtypeStruct((m, LANES), F32)],
        scratch_shapes=[pltpu.VMEM((tm, d), BF16)],
        compiler_params=_cparams(("arbitrary", "arbitrary")),
        name="in_proj",
    )(x, norm_w, w_main, w_gates)


def _rope_kernel(dq_ref, dk_ref, dv_ref, c_ref, s1_ref, s2_ref,
                 q_ref, kf_ref, kb_ref, vb_ref, *, scale):
    c = c_ref[...]
    s1 = s1_ref[...]
    s2 = s2_ref[...]
    half = ROT_DIM // 2
    for g in range(dq_ref.shape[1] // LANES):
        sl = slice(g * LANES, (g + 1) * LANES)
        xq = dq_ref[:, sl]
        rq = xq * c + pltpu.roll(xq, LANES - half, axis=1) * s1 + pltpu.roll(xq, half, axis=1) * s2
        q_ref[:, sl] = (rq * scale).astype(BF16)
        xk = dk_ref[:, sl]
        rk = xk * c + pltpu.roll(xk, LANES - half, axis=1) * s1 + pltpu.roll(xk, half, axis=1) * s2
        kf_ref[:, sl] = rk
        kb_ref[:, sl] = rk.astype(BF16)
    vb_ref[...] = dv_ref[...].astype(BF16)


def _rope_tables(pos):
    half = ROT_DIM // 2
    inv = jnp.exp(-math.log(ROPE_THETA) * jnp.arange(0, ROT_DIM, 2, dtype=F32) / ROT_DIM)
    ang = pos.astype(F32)[:, None] * inv[None, :]
    cos, sin = jnp.cos(ang), jnp.sin(ang)
    n = pos.shape[0]
    ones = jnp.ones((n, LANES - ROT_DIM), F32)
    zeros = jnp.zeros((n, LANES - half), F32)
    c = jnp.concatenate([cos, cos, ones], axis=1)
    s1 = jnp.concatenate([-sin, zeros], axis=1)
    s2 = jnp.concatenate([jnp.zeros((n, half), F32), sin, jnp.zeros((n, LANES - ROT_DIM), F32)], axis=1)
    return c, s1, s2


def _rope(proj, tables, tm, n_tab_blocks, off_q, off_k, off_v, width):
    m = proj.shape[0]
    wb = width
    row = lambda i: (i, 0)
    tab = lambda i: (i % n_tab_blocks, 0)
    outs = pl.pallas_call(
        functools.partial(_rope_kernel, scale=DIFF_DK ** -0.5 * math.log2(math.e)),
        grid=(m // tm,),
        in_specs=[
            pl.BlockSpec((tm, wb), lambda i: (i, off_q // wb)),
            pl.BlockSpec((tm, wb), lambda i: (i, off_k // wb)),
            pl.BlockSpec((tm, wb), lambda i: (i, off_v // wb)),
            pl.BlockSpec((tm, LANES), tab),
            pl.BlockSpec((tm, LANES), tab),
            pl.BlockSpec((tm, LANES), tab),
        ],
        out_specs=[pl.BlockSpec((tm, wb), row)] * 4,
        out_shape=[jax.ShapeDtypeStruct((m, wb), BF16), jax.ShapeDtypeStruct((m, wb), F32),
                   jax.ShapeDtypeStruct((m, wb), BF16), jax.ShapeDtypeStruct((m, wb), BF16)],
        compiler_params=_cparams(("arbitrary",)),
        name="rope",
    )(proj, proj, proj, *tables)
    return outs


def _split2(x):
    hi = x.astype(BF16)
    lo = (x - hi.astype(F32)).astype(BF16)
    return hi, lo


def _gdn_kernel(q_ref, k_ref, v_ref, z_ref, g_ref, hq_ref, hk_ref, hv_ref,
                cwq_ref, cwk_ref, cwv_ref, alog_ref, dtb_ref, nw_ref, s0_ref,
                o_ref, sfin_ref, s_scr, pq_scr, pk_scr, pv_scr, *, chunk):
    h = pl.program_id(1)
    t = pl.program_id(2)
    nt = pl.num_programs(2)
    tt = q_ref.shape[0]
    d = GDN_D
    c = chunk

    @pl.when(t == 0)
    def _():
        s_scr[...] = s0_ref[0]
        pq_scr[...] = hq_ref[0]
        pk_scr[...] = hk_ref[0]
        pv_scr[...] = hv_ref[0]

    def conv_silu(x, prev, w_ref):
        acc = x * w_ref[GDN_CONV - 1:GDN_CONV, :]
        for s in range(1, GDN_CONV):
            acc = acc + _shift_rows(x, prev, s) * w_ref[GDN_CONV - 1 - s:GDN_CONV - s, :]
        return _silu(acc)

    xq = q_ref[...]
    xk = k_ref[...]
    xv = v_ref[...]
    q = conv_silu(xq, pq_scr[...], cwq_ref)
    k = conv_silu(xk, pk_scr[...], cwk_ref)
    v = conv_silu(xv, pv_scr[...], cwv_ref)
    pq_scr[...] = xq[tt - SUBLANES:]
    pk_scr[...] = xk[tt - SUBLANES:]
    pv_scr[...] = xv[tt - SUBLANES:]

    q = q * lax.rsqrt(jnp.sum(q * q, axis=-1, keepdims=True) + EPS) * (d ** -0.5)
    k = k * lax.rsqrt(jnp.sum(k * k, axis=-1, keepdims=True) + EPS)

    gates = g_ref[...]
    xg = gates + dtb_ref[...]
    softplus = jnp.maximum(xg, 0.0) + jnp.log(1.0 + jnp.exp(-jnp.abs(xg)))
    g_all = -jnp.exp(alog_ref[...]) * softplus
    beta_all = _sigmoid(gates)
    lane = lax.broadcasted_iota(jnp.int32, gates.shape, 1)

    ri = lax.broadcasted_iota(jnp.int32, (c, c), 0)
    ci = lax.broadcasted_iota(jnp.int32, (c, c), 1)
    causal = ri >= ci
    strict = ri > ci
    tril_b = causal.astype(BF16)
    eye = (ri == ci).astype(F32)
    rm = lax.broadcasted_iota(jnp.int32, (c, 2 * LANES), 0)
    cm = lax.broadcasted_iota(jnp.int32, (c, 2 * LANES), 1)
    cum_mask = jnp.where(((cm < c) & (rm > cm)) | (cm >= LANES), 1.0, 0.0)

    z = z_ref[...]
    nw = nw_ref[...]
    n_sq = int(math.log2(c)) - 1
    n_chunks = tt // c
    heads = range(2)
    items = [(e, n) for e in heads for n in range(n_chunks)]
    rows = [slice(n * c, (n + 1) * c) for n in range(n_chunks)]

    g_cols, b_cols = [], []
    for e in heads:
        hv = 2 * h + e
        g_cols.append(jnp.sum(jnp.where(lane == hv, g_all, 0.0), axis=-1, keepdims=True))
        b_cols.append(jnp.sum(jnp.where(lane == GDN_HV + hv, beta_all, 0.0), axis=-1, keepdims=True))

    gd = {}
    for e, n in items:
        gc = jnp.broadcast_to(g_cols[e][rows[n]], (c, 2 * LANES)) * cum_mask
        g_hi, g_lo = _split2(gc)
        gd[e, n] = (jnp.dot(tril_b, g_hi, preferred_element_type=F32)
                    + jnp.dot(tril_b, g_lo, preferred_element_type=F32))
    kk = [_bdot_nt(k[r], k[r]) for r in rows]
    qk = [_bdot_nt(q[r], k[r]) for r in rows]

    x, tinv, rhs, intra, qg, kdt, exp_gl = {}, {}, {}, {}, {}, {}, {}
    for e, n in items:
        r = rows[n]
        gb = gd[e, n][:, LANES:]
        decay = jnp.where(causal, jnp.exp(gd[e, n][:, :c]), 0.0)
        exp_g = jnp.exp(gb)
        g_last = gb[c - 1:c, :]
        exp_gl[e, n] = jnp.exp(g_last)
        bb = jnp.broadcast_to(b_cols[e][r], (c, LANES))
        a = jnp.where(strict, -(kk[n] * bb[:, :c] * decay), 0.0)
        x[e, n] = a
        tinv[e, n] = eye + a
        kb = k[r] * bb
        rhs[e, n] = jnp.concatenate([v[r, e * d:(e + 1) * d] * bb, kb * exp_g], axis=1)
        intra[e, n] = jnp.where(causal, qk[n] * decay, 0.0)
        qg[e, n] = q[r] * exp_g
        kdt[e, n] = (k[r] * jnp.exp(g_last - gb)).T

    for it in items:
        x[it] = _bdot(x[it], x[it])
    for _ in range(n_sq - 1):
        for it in items:
            both = _bdot(jnp.concatenate([x[it], tinv[it]], axis=0), x[it])
            x[it] = both[:c]
            tinv[it] = tinv[it] + both[c:]
    for it in items:
        tinv[it] = tinv[it] + _bdot(tinv[it], x[it])

    u, w, kw, ku = {}, {}, {}, {}
    for it in items:
        uw = _bdot(tinv[it], rhs[it])
        u[it], w[it] = uw[:, :d], uw[:, d:]
    for it in items:
        kwu = _bdot(kdt[it], jnp.concatenate([w[it], u[it]], axis=1))
        kw[it], ku[it] = kwu[:, :d], kwu[:, d:]

    s_in = {}
    s_cur = [s_scr[e] for e in heads]
    for n in range(n_chunks):
        for e in heads:
            s_in[e, n] = s_cur[e]
            s_cur[e] = s_cur[e] * exp_gl[e, n] + ku[e, n] - _bdot(kw[e, n], s_cur[e])
    for e in heads:
        s_scr[e] = s_cur[e]

    for it in items:
        e, n = it
        ws = _bdot(jnp.concatenate([w[it], qg[it]], axis=0), s_in[it])
        v_new = u[it] - ws[:c]
        o = ws[c:] + _bdot(intra[it], v_new)
        o = o * lax.rsqrt(jnp.mean(o * o, axis=-1, keepdims=True) + EPS) * nw
        o = o * _silu(z[rows[n], e * d:(e + 1) * d])
        o_ref[rows[n], e * d:(e + 1) * d] = o.astype(o_ref.dtype)

    @pl.when(t == nt - 1)
    def _():
        sfin_ref[0] = s_scr[...]


def _gdn(proj, gates, hist8, conv_w, alog, dtb, nw, s0, nb, t_len, tt, chunk):
    m = proj.shape[0]
    nt = t_len // tt
    hk = GDN_HK
    d = GDN_D
    row = lambda b, h, t: b * nt + t
    ya, sfin = pl.pallas_call(
        functools.partial(_gdn_kernel, chunk=chunk),
        grid=(nb, hk, nt),
        in_specs=[
            pl.BlockSpec((tt, d), lambda b, h, t: (row(b, h, t), h)),
            pl.BlockSpec((tt, d), lambda b, h, t: (row(b, h, t), hk + h)),
            pl.BlockSpec((tt, 2 * d), lambda b, h, t: (row(b, h, t), hk + h)),
            pl.BlockSpec((tt, 2 * d), lambda b, h, t: (row(b, h, t), 2 * hk + h)),
            pl.BlockSpec((tt, LANES), lambda b, h, t: (row(b, h, t), 0)),
            pl.BlockSpec((1, SUBLANES, d), lambda b, h, t: (b, 0, h)),
            pl.BlockSpec((1, SUBLANES, d), lambda b, h, t: (b, 0, hk + h)),
            pl.BlockSpec((1, SUBLANES, 2 * d), lambda b, h, t: (b, 0, hk + h)),
            pl.BlockSpec((GDN_CONV, d), lambda b, h, t: (0, h)),
            pl.BlockSpec((GDN_CONV, d), lambda b, h, t: (0, hk + h)),
            pl.BlockSpec((GDN_CONV, 2 * d), lambda b, h, t: (0, hk + h)),
            pl.BlockSpec((1, LANES), lambda b, h, t: (0, 0)),
            pl.BlockSpec((1, LANES), lambda b, h, t: (0, 0)),
            pl.BlockSpec((1, d), lambda b, h, t: (0, 0)),
            pl.BlockSpec((1, 2, d, d), lambda b, h, t: (b, h, 0, 0)),
        ],
        out_specs=[
            pl.BlockSpec((tt, 2 * d), lambda b, h, t: (row(b, h, t), h)),
            pl.BlockSpec((1, 2, d, d), lambda b, h, t: (b, h, 0, 0)),
        ],
        out_shape=[jax.ShapeDtypeStruct((m, GDN_HV * d), BF16),
                   jax.ShapeDtypeStruct((nb, GDN_HV, d, d), F32)],
        scratch_shapes=[pltpu.VMEM((2, d, d), F32), pltpu.VMEM((SUBLANES, d), F32),
                        pltpu.VMEM((SUBLANES, d), F32), pltpu.VMEM((SUBLANES, 2 * d), F32)],
        compiler_params=_cparams(("arbitrary", "arbitrary", "arbitrary")),
        name="gdn",
    )(proj, proj, proj, proj, gates, hist8, hist8, hist8, conv_w, conv_w, conv_w,
      alog, dtb, nw, s0)
    return ya, sfin


def _attn_kernel(q_ref, k_ref, v_ref, lam_ref, sw_ref, o_ref, acc_ref, m_ref, l_ref,
                 *, tk, tqs, q_off, lam_init):
    i = pl.program_id(2)
    tq = q_ref.shape[0]
    s_len = k_ref.shape[0]
    dk = DIFF_DK
    dv = DIFF_DV
    shift = int(math.log2(CHUNK))
    q0 = q_off + i * tq
    lo_vis = jnp.minimum(((q0 >> shift) + 1) * CHUNK, s_len)
    hi_vis = jnp.minimum((((q0 + tq - 1) >> shift) + 1) * CHUNK, s_len)
    n_full = lo_vis // tk
    n_tot = (hi_vis + tk - 1) // tk
    chains = [(r, c) for r in range(tq // tqs) for c in range(2)]
    lane_tiles = tk // LANES if tk % LANES == 0 else 0

    m_ref[...] = jnp.full(m_ref.shape, NEG_BIG, F32)
    l_ref[...] = jnp.zeros(l_ref.shape, F32)
    acc_ref[...] = jnp.zeros(acc_ref.shape, F32)

    def step(j, masked):
        k0 = pl.multiple_of(j * tk, tk)
        ks = k_ref[pl.ds(k0, tk), :]
        vs = v_ref[pl.ds(k0, tk), :]
        s, m_prev, m_new, p = {}, {}, {}, {}
        for ch in chains:
            r, c = ch
            s[ch] = lax.dot_general(q_ref[r * tqs:(r + 1) * tqs, c * dk:(c + 1) * dk],
                                    ks[:, c * dk:(c + 1) * dk],
                                    (((1,), (1,)), ((), ())), preferred_element_type=F32)
        if masked:
            for ch in chains:
                r, c = ch
                qpos = q0 + r * tqs + lax.broadcasted_iota(jnp.int32, s[ch].shape, 0)
                kpos = k0 + lax.broadcasted_iota(jnp.int32, s[ch].shape, 1)
                s[ch] = jnp.where((kpos >> shift) <= (qpos >> shift), s[ch], NEG_BIG)
        for n, ch in enumerate(chains):
            m_prev[ch] = m_ref[n]
            m_new[ch] = jnp.maximum(m_prev[ch], jnp.max(s[ch], axis=-1, keepdims=True))
        for n, ch in enumerate(chains):
            if lane_tiles:
                m_t = jnp.concatenate([m_new[ch]] * lane_tiles, axis=1)
            else:
                m_t = m_new[ch][:, :1]
            p[ch] = jnp.exp2(s[ch] - m_t)
        for n, ch in enumerate(chains):
            alpha = jnp.exp2(m_prev[ch] - m_new[ch])
            l_ref[n] = alpha * l_ref[n] + jnp.sum(p[ch], axis=-1, keepdims=True)
            pv = jnp.dot(p[ch].astype(BF16), vs, preferred_element_type=F32)
            acc_ref[n] = jnp.concatenate([alpha] * (dv // LANES), axis=1) * acc_ref[n] + pv
            m_ref[n] = m_new[ch]

    def body_full(j, carry):
        step(j, False)
        return carry

    def body_masked(j, carry):
        step(j, True)
        return carry

    lax.fori_loop(0, n_full, body_full, 0)
    lax.fori_loop(n_full, n_tot, body_masked, 0)

    lam = (jnp.exp(jnp.sum(lam_ref[0:1, :] * lam_ref[1:2, :], axis=-1, keepdims=True))
           - jnp.exp(jnp.sum(lam_ref[2:3, :] * lam_ref[3:4, :], axis=-1, keepdims=True)) + lam_init)
    for r in range(tq // tqs):
        l0 = jnp.concatenate([l_ref[2 * r]] * (dv // LANES), axis=1)
        l1 = jnp.concatenate([l_ref[2 * r + 1]] * (dv // LANES), axis=1)
        o = acc_ref[2 * r] / l0 - lam * (acc_ref[2 * r + 1] / l1)
        o = (o * lax.rsqrt(jnp.mean(o * o, axis=-1, keepdims=True) + EPS)) * sw_ref[...]
        o_ref[r * tqs:(r + 1) * tqs, :] = (o * (1.0 - lam_init)).astype(o_ref.dtype)


def _attn(q, k, v, lam4, subln_w, nb, t_len, s_len, tq, tqs, tk, q_off, lam_init):
    m = q.shape[0]
    nq = t_len // tq
    dv = DIFF_DV
    n_chains = 2 * (tq // tqs)
    return pl.pallas_call(
        functools.partial(_attn_kernel, tk=tk, tqs=tqs, q_off=q_off, lam_init=lam_init),
        grid=(nb, DIFF_H, nq),
        in_specs=[
            pl.BlockSpec((tq, dv), lambda b, h, i: (b * nq + i, h)),
            pl.BlockSpec((s_len, dv), lambda b, h, i: (b, h)),
            pl.BlockSpec((s_len, dv), lambda b, h, i: (b, h)),
            pl.BlockSpec((4, DIFF_DK), lambda b, h, i: (0, 0)),
            pl.BlockSpec((1, dv), lambda b, h, i: (0, 0)),
        ],
        out_specs=pl.BlockSpec((tq, dv), lambda b, h, i: (b * nq + i, h)),
        out_shape=jax.ShapeDtypeStruct((m, DIFF_H * dv), BF16),
        scratch_shapes=[pltpu.VMEM((n_chains, tqs, dv), F32), pltpu.VMEM((n_chains, tqs, LANES), F32),
                        pltpu.VMEM((n_chains, tqs, LANES), F32)],
        compiler_params=_cparams(("arbitrary", "arbitrary", "arbitrary")),
        name="diff_attn",
    )(q, k, v, lam4, subln_w)


def _merge_kernel(ya_ref, yb_ref, ga_ref, gb_ref, wa_ref, wb_ref, o_ref):
    pa = jnp.dot(ya_ref[...], wa_ref[...], preferred_element_type=F32)
    pb = jnp.dot(yb_ref[...], wb_ref[...], preferred_element_type=F32)
    o_ref[...] = (_sigmoid(ga_ref[...]) * pa + _sigmoid(gb_ref[...]) * pb).astype(o_ref.dtype)


def _merge(ya, yb, proj, wa, wb, off_ga, off_gb, tm, tn):
    m = ya.shape[0]
    n = wa.shape[1]
    return pl.pallas_call(
        _merge_kernel,
        grid=(m // tm, n // tn),
        in_specs=[
            pl.BlockSpec((tm, ya.shape[1]), lambda i, j: (i, 0)),
            pl.BlockSpec((tm, yb.shape[1]), lambda i, j: (i, 0)),
            pl.BlockSpec((tm, tn), lambda i, j: (i, off_ga // tn + j)),
            pl.BlockSpec((tm, tn), lambda i, j: (i, off_gb // tn + j)),
            pl.BlockSpec((wa.shape[0], tn), lambda i, j: (0, j)),
            pl.BlockSpec((wb.shape[0], tn), lambda i, j: (0, j)),
        ],
        out_specs=pl.BlockSpec((tm, tn), lambda i, j: (i, j)),
        out_shape=jax.ShapeDtypeStruct((m, n), BF16),
        compiler_params=_cparams(("arbitrary", "arbitrary")),
        name="merge",
    )(ya, yb, proj, proj, wa, wb)


def _outproj_kernel(mg_ref, x_ref, w_ref, o_ref):
    o_ref[...] = x_ref[...] + jnp.dot(mg_ref[...], w_ref[...], preferred_element_type=F32)


def _outproj(merged, x, w, tm, tn):
    m, d = x.shape
    return pl.pallas_call(
        _outproj_kernel,
        grid=(m // tm, d // tn),
        in_specs=[
            pl.BlockSpec((tm, merged.shape[1]), lambda i, j: (i, 0)),
            pl.BlockSpec((tm, tn), lambda i, j: (i, j)),
            pl.BlockSpec((w.shape[0], tn), lambda i, j: (0, j)),
        ],
        out_specs=pl.BlockSpec((tm, tn), lambda i, j: (i, j)),
        out_shape=jax.ShapeDtypeStruct((m, d), F32),
        compiler_params=_cparams(("arbitrary", "arbitrary")),
        name="out_proj",
    )(merged, x, w)


def _ffn_kernel(h_ref, nw_ref, wg_ref, wu_ref, cw_ref, cb_ref, wd_ref, hist_ref, nf_ref,
                y_ref, tail_ref, hn_scr, acc_scr, carry_scr, *, n_seq, tiles_per_seq,
                final_norm):
    i = pl.program_id(0)
    j = pl.program_id(1)
    nj = pl.num_programs(1)
    tm = h_ref.shape[0]
    tf = wg_ref.shape[1]
    ts = tm // n_seq

    @pl.when(j == 0)
    def _():
        hh = h_ref[...]
        ms = jnp.mean(hh * hh, axis=-1, keepdims=True)
        hn_scr[...] = ((hh * lax.rsqrt(ms + EPS)) * nw_ref[...]).astype(BF16)
        acc_scr[...] = jnp.zeros(acc_scr.shape, F32)

    hn = hn_scr[...]
    g = jnp.dot(hn, wg_ref[...], preferred_element_type=F32)
    up = jnp.dot(hn, wu_ref[...], preferred_element_type=F32)

    @pl.when((i % tiles_per_seq) == 0)
    def _():
        carry_scr[j] = hist_ref[...]

    w0 = cw_ref[0:1, :]
    w1 = cw_ref[1:2, :]
    w2 = cw_ref[2:3, :]
    cb = cb_ref[...]
    outs = []
    for s in range(n_seq):
        gs = g[s * ts:(s + 1) * ts]
        prev = carry_scr[j, s]
        conv = gs * w2 + _shift_rows(gs, prev, 1) * w1 + _shift_rows(gs, prev, 2) * w0 + cb
        outs.append(_silu(conv) * up[s * ts:(s + 1) * ts])
        tail = gs[ts - SUBLANES:]
        carry_scr[j, s] = tail
        tail_ref[s] = tail
    act = outs[0] if n_seq == 1 else jnp.concatenate(outs, axis=0)
    acc_scr[...] += jnp.dot(act.astype(BF16), wd_ref[...], preferred_element_type=F32)

    @pl.when(j == nj - 1)
    def _():
        o = h_ref[...] + acc_scr[...]
        if final_norm:
            ms = jnp.mean(o * o, axis=-1, keepdims=True)
            o = (o * lax.rsqrt(ms + EPS)) * nf_ref[...]
        y_ref[...] = o


def _ffn(h, norm_w, wg, wu, conv_w, conv_b, wd, hist8, norm_final, final_norm, t_len, tm, tf):
    m, d = h.shape
    f = wg.shape[1]
    nj = f // tf
    if tm >= t_len:
        n_seq, tiles_per_seq = tm // t_len, 1
    else:
        n_seq, tiles_per_seq = 1, t_len // tm
    n_tiles = m // tm
    seq_blk = lambda i, j: ((i // tiles_per_seq), 0, j)
    y, tails = pl.pallas_call(
        functools.partial(_ffn_kernel, n_seq=n_seq, tiles_per_seq=tiles_per_seq, final_norm=final_norm),
        grid=(n_tiles, nj),
        in_specs=[
            pl.BlockSpec((tm, d), lambda i, j: (i, 0)),
            pl.BlockSpec((1, d), lambda i, j: (0, 0)),
            pl.BlockSpec((d, tf), lambda i, j: (0, j)),
            pl.BlockSpec((d, tf), lambda i, j: (0, j)),
            pl.BlockSpec((FFN_CONV, tf), lambda i, j: (0, j)),
            pl.BlockSpec((1, tf), lambda i, j: (0, j)),
            pl.BlockSpec((tf, d), lambda i, j: (j, 0)),
            pl.BlockSpec((n_seq, SUBLANES, tf), seq_blk),
            pl.BlockSpec((1, d), lambda i, j: (0, 0)),
        ],
        out_specs=[
            pl.BlockSpec((tm, d), lambda i, j: (i, 0)),
            pl.BlockSpec((n_seq, SUBLANES, tf), lambda i, j: (i, 0, j)),
        ],
        out_shape=[jax.ShapeDtypeStruct((m, d), F32),
                   jax.ShapeDtypeStruct((n_tiles * n_seq, SUBLANES, f), F32)],
        scratch_shapes=[pltpu.VMEM((tm, d), BF16), pltpu.VMEM((tm, d), F32),
                        pltpu.VMEM((nj, n_seq, SUBLANES, tf), F32)],
        compiler_params=_cparams(("arbitrary", "arbitrary")),
        name="conv_ffn",
    )(h, norm_w, wg, wu, conv_w, conv_b, wd, hist8, norm_final)
    return y, tails


def _pad_hist(hist):
    b, r, c = hist.shape
    return jnp.concatenate([jnp.zeros((b, SUBLANES - r, c), hist.dtype), hist], axis=1)


def _pick(n, cands):
    for c in cands:
        if n % c == 0:
            return c
    return n


def _layer(x, pos_offset, k_hist, v_hist, gdn_conv_hist, gdn_s0, ffn_hist, lam_init, wts, norm_final,
           final_norm):
    nb, t_len, d_model = x.shape
    m = nb * t_len
    past = k_hist.shape[1]
    x2 = x.reshape(m, d_model)
    qk_dim = GDN_HK * GDN_D
    v_dim = GDN_HV * GDN_D
    dqk = DIFF_H * 2 * DIFF_DK
    off = {"gq": 0, "gk": qk_dim, "gv": 2 * qk_dim, "gz": 2 * qk_dim + v_dim}
    off["dq"] = off["gz"] + v_dim
    off["dk"] = off["dq"] + dqk
    off["dv"] = off["dk"] + dqk
    off["ga"] = off["dv"] + DIFF_H * DIFF_DV
    off["gb"] = off["ga"] + d_model

    tm = _pick(m, (1024, 512, 256))
    tn = _pick(wts["w_main"].shape[1], (1024, 512, 256, 128))
    proj, gates = _inproj(x2, wts["norm_mix"], wts["w_main"], wts["w_gates"], tm, tn)

    tmr = _pick(m, (512, 256))
    pos = pos_offset + jnp.arange(t_len, dtype=jnp.int32)
    tables = _rope_tables(pos)
    if t_len < tmr:
        tables = [jnp.tile(tb, (tmr // t_len, 1)) for tb in tables]
    n_tab = max(t_len // tmr, 1)
    q_r, k_rows, k_b, v_b = _rope(proj, tables, tmr, n_tab, off["dq"], off["dk"], off["dv"], dqk)
    v_rows = proj[:, off["dv"]:off["dv"] + DIFF_H * DIFF_DV]

    chunk = CHUNK if t_len % CHUNK == 0 else t_len
    tt = _pick(t_len, (256,)) if t_len % CHUNK == 0 else t_len
    ya, gdn_s = _gdn(proj, gates, _pad_hist(gdn_conv_hist), wts["gdn_conv_w"], wts["alog"], wts["dtb"],
                     wts["gdn_norm_w"], gdn_s0, nb, t_len, tt, chunk)
    qkv_tail = proj.reshape(nb, t_len, -1)[:, t_len - (GDN_CONV - 1):, :off["gz"]]
    if t_len >= GDN_CONV - 1:
        gdn_conv_new = qkv_tail
    else:
        gdn_conv_new = jnp.concatenate([gdn_conv_hist, proj.reshape(nb, t_len, -1)[:, :, :off["gz"]]],
                                       axis=1)[:, -(GDN_CONV - 1):]

    s_len = past + t_len
    if past:
        k_all = jnp.concatenate([k_hist.reshape(nb, past, dqk).astype(BF16), k_b.reshape(nb, t_len, dqk)],
                                axis=1).reshape(nb * s_len, dqk)
        v_all = jnp.concatenate([v_hist.reshape(nb, past, dqk).astype(BF16), v_b.reshape(nb, t_len, dqk)],
                                axis=1).reshape(nb * s_len, dqk)
    else:
        k_all, v_all = k_b, v_b
    tq = _pick(t_len, (512, 256))
    tqs = _pick(tq, (256,))
    tk = _pick(s_len, (512, 256))
    yb = _attn(q_r, k_all, v_all, wts["lam4"], wts["subln_w"], nb, t_len, s_len, tq, tqs, tk,
               pos_offset, lam_init)

    tmm = _pick(m, (512, 256))
    tnm = _pick(d_model, (512, 256, 128))
    merged = _merge(ya, yb, proj, wts["w_br_a"], wts["w_br_b"], off["ga"], off["gb"], tmm, tnm)
    h = _outproj(merged, x2, wts["w_out"], tmm, tnm)
    tmf = _pick(m, (512, 256))
    y, tails = _ffn(h, wts["norm_ffn"], wts["w_gate"], wts["w_up"], wts["ffn_conv_w"], wts["ffn_conv_b"],
                    wts["w_down"], _pad_hist(ffn_hist), norm_final, final_norm, t_len, tmf, 512)
    d_ff = tails.shape[-1]
    if tmf >= t_len:
        seq_tails = tails
    else:
        seq_tails = tails.reshape(nb, t_len // tmf, SUBLANES, d_ff)[:, -1]
    ffn_new = seq_tails[:, SUBLANES - (FFN_CONV - 1):]

    return (y.reshape(nb, t_len, d_model),
            k_rows.reshape(nb, t_len, DIFF_H, 2 * DIFF_DK),
            v_rows.reshape(nb, t_len, DIFF_H, DIFF_DV),
            gdn_conv_new, gdn_s, ffn_new)


def _prep_weights(l, norm_mix, w_in, gdn_conv_w, gdn_a_log, gdn_dt_bias, gdn_norm_w,
                  lq1, lk1, lq2, lk2, subln_w, w_branch, w_out, norm_ffn,
                  w_gate, w_up, ffn_conv_w, ffn_conv_b, w_down):
    qk_dim = GDN_HK * GDN_D
    v_dim = GDN_HV * GDN_D
    g0 = 2 * qk_dim + 2 * v_dim
    g1 = g0 + 2 * GDN_HV
    wi = w_in[l]
    d_model = wi.shape[0]
    w_main = jnp.concatenate([wi[:, :g0], wi[:, g1:]], axis=1).astype(BF16)
    w_gates = jnp.concatenate([wi[:, g0:g1], jnp.zeros((d_model, LANES - 2 * GDN_HV), wi.dtype)],
                              axis=1).astype(BF16)
    pad = jnp.zeros((LANES - GDN_HV,), F32)
    return {
        "norm_mix": norm_mix[l].reshape(1, -1),
        "w_main": w_main,
        "w_gates": w_gates,
        "gdn_conv_w": gdn_conv_w[l],
        "alog": jnp.concatenate([gdn_a_log[l].astype(F32), pad]).reshape(1, LANES),
        "dtb": jnp.concatenate([gdn_dt_bias[l].astype(F32), pad]).reshape(1, LANES),
        "gdn_norm_w": gdn_norm_w[l].reshape(1, -1),
        "lam4": jnp.stack([lq1[l], lk1[l], lq2[l], lk2[l]], axis=0).astype(F32),
        "subln_w": subln_w[l].reshape(1, -1),
        "w_br_a": w_branch[l][:v_dim].astype(BF16),
        "w_br_b": w_branch[l][v_dim:].astype(BF16),
        "w_out": w_out[l].astype(BF16),
        "norm_ffn": norm_ffn[l].reshape(1, -1),
        "w_gate": w_gate[l].astype(BF16),
        "w_up": w_up[l].astype(BF16),
        "ffn_conv_w": ffn_conv_w[l],
        "ffn_conv_b": ffn_conv_b[l].reshape(1, -1),
        "w_down": w_down[l].astype(BF16),
    }


def kernel(x_prompt, x_sample, cache_diff_k, cache_diff_v, state_gdn_conv, state_gdn_rec, state_ffn_conv,
           norm_mix, w_in, gdn_conv_w, gdn_a_log, gdn_dt_bias, gdn_norm_w,
           diff_lambda_q1, diff_lambda_k1, diff_lambda_q2, diff_lambda_k2, diff_subln_w,
           w_branch, w_out, norm_ffn, ffn_w_gate, ffn_w_up, ffn_conv_w, ffn_conv_b, ffn_w_down,
           norm_final):
    depth = w_in.shape[0]
    bp = x_prompt.shape[0]
    past = cache_diff_k.shape[2]
    dt = x_prompt.dtype
    d_ff = ffn_w_gate.shape[-1]
    conv_dim = gdn_conv_w.shape[-1]
    nf = norm_final.reshape(1, -1)
    hp, hs = x_prompt, x_sample
    p_out = [[], [], [], [], []]
    s_out = [[], [], [], [], []]
    for l in range(depth):
        lam_init = 0.8 - 0.6 * math.exp(-0.3 * l)
        wts = _prep_weights(l, norm_mix, w_in, gdn_conv_w, gdn_a_log, gdn_dt_bias, gdn_norm_w,
                            diff_lambda_q1, diff_lambda_k1, diff_lambda_q2, diff_lambda_k2, diff_subln_w,
                            w_branch, w_out, norm_ffn, ffn_w_gate, ffn_w_up, ffn_conv_w, ffn_conv_b,
                            ffn_w_down)
        last = l == depth - 1
        hp, *sp = _layer(hp, 0,
                         jnp.zeros((bp, 0, DIFF_H, 2 * DIFF_DK), dt),
                         jnp.zeros((bp, 0, DIFF_H, DIFF_DV), dt),
                         jnp.zeros((bp, GDN_CONV - 1, conv_dim), dt),
                         jnp.zeros((bp, GDN_HV, GDN_D, GDN_D), dt),
                         jnp.zeros((bp, FFN_CONV - 1, d_ff), dt),
                         lam_init, wts, nf, last)
        hs, *ss = _layer(hs, past, cache_diff_k[l], cache_diff_v[l], state_gdn_conv[l], state_gdn_rec[l],
                         state_ffn_conv[l], lam_init, wts, nf, last)
        for i in range(5):
            p_out[i].append(sp[i])
            s_out[i].append(ss[i])
    p_k, p_v, p_gdn_conv, p_gdn_rec, p_ffn_conv = [jnp.stack(a, axis=0) for a in p_out]
    s_k, s_v, s_gdn_conv, s_gdn_rec, s_ffn_conv = [jnp.stack(a, axis=0) for a in s_out]
    return (hp, hs, p_k, p_v, p_gdn_conv, p_gdn_rec, p_ffn_conv,
            s_k, s_v, s_gdn_conv, s_gdn_rec, s_ffn_conv)
```

```python
import functools
import math

import jax
import jax.numpy as jnp
from jax import lax
from jax.experimental import pallas as pl
from jax.experimental.pallas import tpu as pltpu

F32 = jnp.float32
BF16 = jnp.bfloat16

CHUNK = 64
EPS = 1e-6
GDN_HK = 16
GDN_HV = 32
GDN_D = 128
GDN_CONV = 4
DIFF_H = 8
DIFF_DK = 128
DIFF_DV = 2 * DIFF_DK
ROPE_THETA = 500000.0
ROT_DIM = DIFF_DK // 4
FFN_CONV = 3
LANES = 128
SUBLANES = 8
NEG_BIG = -1e30
VMEM_LIMIT = 56 * 1024 * 1024


def _cparams(sem):
    return pltpu.CompilerParams(dimension_semantics=sem, vmem_limit_bytes=VMEM_LIMIT)


def _sigmoid(x):
    return 1.0 / (1.0 + jnp.exp(-x))


def _silu(x):
    return x * _sigmoid(x)


def _bdot(a, b):
    return jnp.dot(a.astype(BF16), b.astype(BF16), preferred_element_type=F32)


def _bdot_nt(a, b):
    return lax.dot_general(a.astype(BF16), b.astype(BF16), (((1,), (1,)), ((), ())),
                           preferred_element_type=F32)


def _shift_rows(x, prev8, s):
    n = x.shape[0]
    xr = pltpu.roll(x, s, axis=0)
    pr = pltpu.roll(prev8, s, axis=0)
    row = lax.broadcasted_iota(jnp.int32, pr.shape, 0)
    head = jnp.where(row < s, pr, xr[:SUBLANES])
    if n == SUBLANES:
        return head
    return jnp.concatenate([head, xr[SUBLANES:]], axis=0)


def _inproj_kernel(x_ref, nw_ref, w_ref, wg_ref, o_ref, og_ref, xn_ref):
    @pl.when(pl.program_id(1) == 0)
    def _():
        x = x_ref[...]
        ms = jnp.mean(x * x, axis=-1, keepdims=True)
        xn = ((x * lax.rsqrt(ms + EPS)) * nw_ref[...]).astype(BF16)
        xn_ref[...] = xn
        og_ref[...] = jnp.dot(xn, wg_ref[...], preferred_element_type=F32)

    o_ref[...] = jnp.dot(xn_ref[...], w_ref[...], preferred_element_type=F32)


def _inproj(x, norm_w, w_main, w_gates, tm, tn):
    m, d = x.shape
    n = w_main.shape[1]
    return pl.pallas_call(
        _inproj_kernel,
        grid=(m // tm, n // tn),
        in_specs=[
            pl.BlockSpec((tm, d), lambda i, j: (i, 0)),
            pl.BlockSpec((1, d), lambda i, j: (0, 0)),
            pl.BlockSpec((d, tn), lambda i, j: (0, j)),
            pl.BlockSpec((d, LANES), lambda i, j: (0, 0)),
        ],
        out_specs=[
            pl.BlockSpec((tm, tn), lambda i, j: (i, j)),
            pl.BlockSpec((tm, LANES), lambda i, j: (i, 0)),
        ],
        out_shape=[jax.ShapeDtypeStruct((m, n), F32), jax.ShapeDtypeStruct((m, LANES), F32)],
        scratch_shapes=[pltpu.VMEM((tm, d), BF16)],
        compiler_params=_cparams(("arbitrary", "arbitrary")),
        name="in_proj",
    )(x, norm_w, w_main, w_gates)


def _rope_kernel(dq_ref, dk_ref, dv_ref, c_ref, s1_ref, s2_ref,
                 q_ref, kf_ref, kb_ref, vf_ref, vb_ref, *, scale):
    c = c_ref[...]
    s1 = s1_ref[...]
    s2 = s2_ref[...]
    half = ROT_DIM // 2
    for g in range(dq_ref.shape[1] // LANES):
        sl = slice(g * LANES, (g + 1) * LANES)
        xq = dq_ref[:, sl]
        rq = xq * c + pltpu.roll(xq, LANES - half, axis=1) * s1 + pltpu.roll(xq, half, axis=1) * s2
        q_ref[:, sl] = (rq * scale).astype(BF16)
        xk = dk_ref[:, sl]
        rk = xk * c + pltpu.roll(xk, LANES - half, axis=1) * s1 + pltpu.roll(xk, half, axis=1) * s2
        kf_ref[:, sl] = rk
        kb_ref[:, sl] = rk.astype(BF16)
    xv = dv_ref[...]
    vf_ref[...] = xv
    vb_ref[...] = xv.astype(BF16)


def _rope_tables(pos):
    half = ROT_DIM // 2
    inv = jnp.exp(-math.log(ROPE_THETA) * jnp.arange(0, ROT_DIM, 2, dtype=F32) / ROT_DIM)
    ang = pos.astype(F32)[:, None] * inv[None, :]
    cos, sin = jnp.cos(ang), jnp.sin(ang)
    n = pos.shape[0]
    ones = jnp.ones((n, LANES - ROT_DIM), F32)
    zeros = jnp.zeros((n, LANES - half), F32)
    c = jnp.concatenate([cos, cos, ones], axis=1)
    s1 = jnp.concatenate([-sin, zeros], axis=1)
    s2 = jnp.concatenate([jnp.zeros((n, half), F32), sin, jnp.zeros((n, LANES - ROT_DIM), F32)], axis=1)
    return c, s1, s2


def _rope(proj, tables, tm, n_tab_blocks, off_q, off_k, off_v, width):
    m = proj.shape[0]
    wb = width
    row = lambda i: (i, 0)
    tab = lambda i: (i % n_tab_blocks, 0)
    outs = pl.pallas_call(
        functools.partial(_rope_kernel, scale=DIFF_DK ** -0.5 * math.log2(math.e)),
        grid=(m // tm,),
        in_specs=[
            pl.BlockSpec((tm, wb), lambda i: (i, off_q // wb)),
            pl.BlockSpec((tm, wb), lambda i: (i, off_k // wb)),
            pl.BlockSpec((tm, wb), lambda i: (i, off_v // wb)),
            pl.BlockSpec((tm, LANES), tab),
            pl.BlockSpec((tm, LANES), tab),
            pl.BlockSpec((tm, LANES), tab),
        ],
        out_specs=[pl.BlockSpec((tm, wb), row)] * 5,
        out_shape=[jax.ShapeDtypeStruct((m, wb), BF16), jax.ShapeDtypeStruct((m, wb), F32),
                   jax.ShapeDtypeStruct((m, wb), BF16), jax.ShapeDtypeStruct((m, wb), F32),
                   jax.ShapeDtypeStruct((m, wb), BF16)],
        compiler_params=_cparams(("arbitrary",)),
        name="rope",
    )(proj, proj, proj, *tables)
    return outs


def _split2(x):
    hi = x.astype(BF16)
    lo = (x - hi.astype(F32)).astype(BF16)
    return hi, lo


def _gdn_kernel(q_ref, k_ref, v_ref, z_ref, g_ref, hq_ref, hk_ref, hv_ref,
                cwq_ref, cwk_ref, cwv_ref, alog_ref, dtb_ref, nw_ref, s0_ref,
                o_ref, sfin_ref, s_scr, pq_scr, pk_scr, pv_scr, *, chunk, kg):
    h = pl.program_id(1)
    t = pl.program_id(2)
    nt = pl.num_programs(2)
    tt = q_ref.shape[0]
    d = GDN_D
    c = chunk

    @pl.when(t == 0)
    def _():
        s_scr[...] = s0_ref[0]
        pq_scr[...] = hq_ref[0]
        pk_scr[...] = hk_ref[0]
        pv_scr[...] = hv_ref[0]

    def conv_silu(x, prev, w_ref):
        acc = x * w_ref[GDN_CONV - 1:GDN_CONV, :]
        for s in range(1, GDN_CONV):
            acc = acc + _shift_rows(x, prev, s) * w_ref[GDN_CONV - 1 - s:GDN_CONV - s, :]
        return _silu(acc)

    xq = q_ref[...]
    xk = k_ref[...]
    xv = v_ref[...]
    qc = conv_silu(xq, pq_scr[...], cwq_ref)
    kc = conv_silu(xk, pk_scr[...], cwk_ref)
    v = conv_silu(xv, pv_scr[...], cwv_ref)
    pq_scr[...] = xq[tt - SUBLANES:]
    pk_scr[...] = xk[tt - SUBLANES:]
    pv_scr[...] = xv[tt - SUBLANES:]

    q, k = [], []
    for kh in range(kg):
        qh = qc[:, kh * d:(kh + 1) * d]
        kh_ = kc[:, kh * d:(kh + 1) * d]
        q.append(qh * lax.rsqrt(jnp.sum(qh * qh, axis=-1, keepdims=True) + EPS) * (d ** -0.5))
        k.append(kh_ * lax.rsqrt(jnp.sum(kh_ * kh_, axis=-1, keepdims=True) + EPS))

    gates = g_ref[...]
    xg = gates + dtb_ref[...]
    softplus = jnp.maximum(xg, 0.0) + jnp.log(1.0 + jnp.exp(-jnp.abs(xg)))
    g_all = -jnp.exp(alog_ref[...]) * softplus
    beta_all = _sigmoid(gates)
    lane = lax.broadcasted_iota(jnp.int32, gates.shape, 1)

    ri = lax.broadcasted_iota(jnp.int32, (c, c), 0)
    ci = lax.broadcasted_iota(jnp.int32, (c, c), 1)
    causal = ri >= ci
    strict = ri > ci
    tril_b = causal.astype(BF16)
    eye = (ri == ci).astype(F32)
    rm = lax.broadcasted_iota(jnp.int32, (c, 2 * LANES), 0)
    cm = lax.broadcasted_iota(jnp.int32, (c, 2 * LANES), 1)
    cum_mask = jnp.where(((cm < c) & (rm > cm)) | (cm >= LANES), 1.0, 0.0)

    z = z_ref[...]
    nw = nw_ref[...]
    n_sq = int(math.log2(c)) - 1
    n_chunks = tt // c
    heads = range(2 * kg)
    items = [(e, n) for e in heads for n in range(n_chunks)]
    rows = [slice(n * c, (n + 1) * c) for n in range(n_chunks)]

    g_cols, b_cols = [], []
    for e in heads:
        hv = 2 * kg * h + e
        g_cols.append(jnp.sum(jnp.where(lane == hv, g_all, 0.0), axis=-1, keepdims=True))
        b_cols.append(jnp.sum(jnp.where(lane == GDN_HV + hv, beta_all, 0.0), axis=-1, keepdims=True))

    gd = {}
    for e, n in items:
        gc = jnp.broadcast_to(g_cols[e][rows[n]], (c, 2 * LANES)) * cum_mask
        g_hi, g_lo = _split2(gc)
        gd[e, n] = (jnp.dot(tril_b, g_hi, preferred_element_type=F32)
                    + jnp.dot(tril_b, g_lo, preferred_element_type=F32))
    kk = {(kh, n): _bdot_nt(k[kh][r], k[kh][r]) for kh in range(kg) for n, r in enumerate(rows)}
    qk = {(kh, n): _bdot_nt(q[kh][r], k[kh][r]) for kh in range(kg) for n, r in enumerate(rows)}

    x, tinv, rhs, intra, qg, kdt, exp_gl = {}, {}, {}, {}, {}, {}, {}
    for e, n in items:
        r = rows[n]
        kh = e // 2
        gb = gd[e, n][:, LANES:]
        decay = jnp.where(causal, jnp.exp(gd[e, n][:, :c]), 0.0)
        exp_g = jnp.exp(gb)
        g_last = gb[c - 1:c, :]
        exp_gl[e, n] = jnp.exp(g_last)
        bb = jnp.broadcast_to(b_cols[e][r], (c, LANES))
        a = jnp.where(strict, -(kk[kh, n] * bb[:, :c] * decay), 0.0)
        x[e, n] = a
        tinv[e, n] = eye + a
        kb = k[kh][r] * bb
        rhs[e, n] = jnp.concatenate([v[r, e * d:(e + 1) * d] * bb, kb * exp_g], axis=1)
        intra[e, n] = jnp.where(causal, qk[kh, n] * decay, 0.0)
        qg[e, n] = q[kh][r] * exp_g
        kdt[e, n] = (k[kh][r] * jnp.exp(g_last - gb)).T

    for it in items:
        x[it] = _bdot(x[it], x[it])
    for _ in range(n_sq - 1):
        for it in items:
            both = _bdot(jnp.concatenate([x[it], tinv[it]], axis=0), x[it])
            x[it] = both[:c]
            tinv[it] = tinv[it] + both[c:]
    for it in items:
        tinv[it] = tinv[it] + _bdot(tinv[it], x[it])

    u, w, kw, ku = {}, {}, {}, {}
    for it in items:
        uw = _bdot(tinv[it], rhs[it])
        u[it], w[it] = uw[:, :d], uw[:, d:]
    for it in items:
        kwu = _bdot(kdt[it], jnp.concatenate([w[it], u[it]], axis=1))
        kw[it], ku[it] = kwu[:, :d], kwu[:, d:]

    s_in = {}
    s_cur = [s_scr[e] for e in heads]
    for n in range(n_chunks):
        for e in heads:
            s_in[e, n] = s_cur[e]
            s_cur[e] = s_cur[e] * exp_gl[e, n] + ku[e, n] - _bdot(kw[e, n], s_cur[e])
    for e in heads:
        s_scr[e] = s_cur[e]

    for it in items:
        e, n = it
        ws = _bdot(jnp.concatenate([w[it], qg[it]], axis=0), s_in[it])
        v_new = u[it] - ws[:c]
        o = ws[c:] + _bdot(intra[it], v_new)
        o = o * lax.rsqrt(jnp.mean(o * o, axis=-1, keepdims=True) + EPS) * nw
        o = o * _silu(z[rows[n], e * d:(e + 1) * d])
        o_ref[rows[n], e * d:(e + 1) * d] = o.astype(o_ref.dtype)

    @pl.when(t == nt - 1)
    def _():
        sfin_ref[0] = s_scr[...]


def _gdn(proj, gates, hist8, conv_w, alog, dtb, nw, s0, nb, t_len, tt, chunk, kg):
    m = proj.shape[0]
    nt = t_len // tt
    d = GDN_D
    wqk = kg * d
    wv = 2 * kg * d
    nhb = GDN_HK // kg
    row = lambda b, h, t: b * nt + t
    ya, sfin = pl.pallas_call(
        functools.partial(_gdn_kernel, chunk=chunk, kg=kg),
        grid=(nb, nhb, nt),
        in_specs=[
            pl.BlockSpec((tt, wqk), lambda b, h, t: (row(b, h, t), h)),
            pl.BlockSpec((tt, wqk), lambda b, h, t: (row(b, h, t), nhb + h)),
            pl.BlockSpec((tt, wv), lambda b, h, t: (row(b, h, t), nhb + h)),
            pl.BlockSpec((tt, wv), lambda b, h, t: (row(b, h, t), 2 * nhb + h)),
            pl.BlockSpec((tt, LANES), lambda b, h, t: (row(b, h, t), 0)),
            pl.BlockSpec((1, SUBLANES, wqk), lambda b, h, t: (b, 0, h)),
            pl.BlockSpec((1, SUBLANES, wqk), lambda b, h, t: (b, 0, nhb + h)),
            pl.BlockSpec((1, SUBLANES, wv), lambda b, h, t: (b, 0, nhb + h)),
            pl.BlockSpec((GDN_CONV, wqk), lambda b, h, t: (0, h)),
            pl.BlockSpec((GDN_CONV, wqk), lambda b, h, t: (0, nhb + h)),
            pl.BlockSpec((GDN_CONV, wv), lambda b, h, t: (0, nhb + h)),
            pl.BlockSpec((1, LANES), lambda b, h, t: (0, 0)),
            pl.BlockSpec((1, LANES), lambda b, h, t: (0, 0)),
            pl.BlockSpec((1, d), lambda b, h, t: (0, 0)),
            pl.BlockSpec((1, 2 * kg, d, d), lambda b, h, t: (b, h, 0, 0)),
        ],
        out_specs=[
            pl.BlockSpec((tt, wv), lambda b, h, t: (row(b, h, t), h)),
            pl.BlockSpec((1, 2 * kg, d, d), lambda b, h, t: (b, h, 0, 0)),
        ],
        out_shape=[jax.ShapeDtypeStruct((m, GDN_HV * d), BF16),
                   jax.ShapeDtypeStruct((nb, GDN_HV, d, d), F32)],
        scratch_shapes=[pltpu.VMEM((2 * kg, d, d), F32), pltpu.VMEM((SUBLANES, wqk), F32),
                        pltpu.VMEM((SUBLANES, wqk), F32), pltpu.VMEM((SUBLANES, wv), F32)],
        compiler_params=_cparams(("arbitrary", "arbitrary", "arbitrary")),
        name="gdn",
    )(proj, proj, proj, proj, gates, hist8, hist8, hist8, conv_w, conv_w, conv_w,
      alog, dtb, nw, s0)
    return ya, sfin


def _attn_kernel(q_ref, k_ref, v_ref, lam_ref, sw_ref, o_ref, acc_ref, m_ref, l_ref,
                 *, tk, tqs, q_off, lam_init):
    i = pl.program_id(2)
    tq = q_ref.shape[0]
    s_len = k_ref.shape[0]
    dk = DIFF_DK
    dv = DIFF_DV
    shift = int(math.log2(CHUNK))
    q0 = q_off + i * tq
    lo_vis = jnp.minimum(((q0 >> shift) + 1) * CHUNK, s_len)
    hi_vis = jnp.minimum((((q0 + tq - 1) >> shift) + 1) * CHUNK, s_len)
    n_full = lo_vis // tk
    n_tot = (hi_vis + tk - 1) // tk
    chains = [(r, c) for r in range(tq // tqs) for c in range(2)]
    lane_tiles = tk // LANES if tk % LANES == 0 else 0

    m_ref[...] = jnp.full(m_ref.shape, NEG_BIG, F32)
    l_ref[...] = jnp.zeros(l_ref.shape, F32)
    acc_ref[...] = jnp.zeros(acc_ref.shape, F32)

    def step(j, masked):
        k0 = pl.multiple_of(j * tk, tk)
        ks = k_ref[pl.ds(k0, tk), :]
        vs = v_ref[pl.ds(k0, tk), :]
        s, m_prev, m_new, p = {}, {}, {}, {}
        for ch in chains:
            r, c = ch
            s[ch] = lax.dot_general(q_ref[r * tqs:(r + 1) * tqs, c * dk:(c + 1) * dk],
                                    ks[:, c * dk:(c + 1) * dk],
                                    (((1,), (1,)), ((), ())), preferred_element_type=F32)
        if masked:
            for ch in chains:
                r, c = ch
                qpos = q0 + r * tqs + lax.broadcasted_iota(jnp.int32, s[ch].shape, 0)
                kpos = k0 + lax.broadcasted_iota(jnp.int32, s[ch].shape, 1)
                s[ch] = jnp.where((kpos >> shift) <= (qpos >> shift), s[ch], NEG_BIG)
        for n, ch in enumerate(chains):
            m_prev[ch] = m_ref[n]
            m_new[ch] = jnp.maximum(m_prev[ch], jnp.max(s[ch], axis=-1, keepdims=True))
        for n, ch in enumerate(chains):
            if lane_tiles:
                m_t = jnp.concatenate([m_new[ch]] * lane_tiles, axis=1)
            else:
                m_t = m_new[ch][:, :1]
            p[ch] = jnp.exp2(s[ch] - m_t)
        for n, ch in enumerate(chains):
            alpha = jnp.exp2(m_prev[ch] - m_new[ch])
            l_ref[n] = alpha * l_ref[n] + jnp.sum(p[ch], axis=-1, keepdims=True)
            pv = jnp.dot(p[ch].astype(BF16), vs, preferred_element_type=F32)
            acc_ref[n] = jnp.concatenate([alpha] * (dv // LANES), axis=1) * acc_ref[n] + pv
            m_ref[n] = m_new[ch]

    def body_full(j, carry):
        step(j, False)
        return carry

    def body_masked(j, carry):
        step(j, True)
        return carry

    lax.fori_loop(0, n_full, body_full, 0)
    lax.fori_loop(n_full, n_tot, body_masked, 0)

    lam = (jnp.exp(jnp.sum(lam_ref[0:1, :] * lam_ref[1:2, :], axis=-1, keepdims=True))
           - jnp.exp(jnp.sum(lam_ref[2:3, :] * lam_ref[3:4, :], axis=-1, keepdims=True)) + lam_init)
    for r in range(tq // tqs):
        l0 = jnp.concatenate([l_ref[2 * r]] * (dv // LANES), axis=1)
        l1 = jnp.concatenate([l_ref[2 * r + 1]] * (dv // LANES), axis=1)
        o = acc_ref[2 * r] / l0 - lam * (acc_ref[2 * r + 1] / l1)
        o = (o * lax.rsqrt(jnp.mean(o * o, axis=-1, keepdims=True) + EPS)) * sw_ref[...]
        o_ref[r * tqs:(r + 1) * tqs, :] = (o * (1.0 - lam_init)).astype(o_ref.dtype)


def _attn(q, k, v, lam4, subln_w, nb, t_len, s_len, tq, tqs, tk, q_off, lam_init):
    m = q.shape[0]
    nq = t_len // tq
    dv = DIFF_DV
    n_chains = 2 * (tq // tqs)
    return pl.pallas_call(
        functools.partial(_attn_kernel, tk=tk, tqs=tqs, q_off=q_off, lam_init=lam_init),
        grid=(nb, DIFF_H, nq),
        in_specs=[
            pl.BlockSpec((tq, dv), lambda b, h, i: (b * nq + i, h)),
            pl.BlockSpec((s_len, dv), lambda b, h, i: (b, h)),
            pl.BlockSpec((s_len, dv), lambda b, h, i: (b, h)),
            pl.BlockSpec((4, DIFF_DK), lambda b, h, i: (0, 0)),
            pl.BlockSpec((1, dv), lambda b, h, i: (0, 0)),
        ],
        out_specs=pl.BlockSpec((tq, dv), lambda b, h, i: (b * nq + i, h)),
        out_shape=jax.ShapeDtypeStruct((m, DIFF_H * dv), BF16),
        scratch_shapes=[pltpu.VMEM((n_chains, tqs, dv), F32), pltpu.VMEM((n_chains, tqs, LANES), F32),
                        pltpu.VMEM((n_chains, tqs, LANES), F32)],
        compiler_params=_cparams(("arbitrary", "arbitrary", "arbitrary")),
        name="diff_attn",
    )(q, k, v, lam4, subln_w)


def _merge_kernel(ya_ref, yb_ref, ga_ref, gb_ref, wa_ref, wb_ref, o_ref):
    pa = jnp.dot(ya_ref[...], wa_ref[...], preferred_element_type=F32)
    pb = jnp.dot(yb_ref[...], wb_ref[...], preferred_element_type=F32)
    o_ref[...] = (_sigmoid(ga_ref[...]) * pa + _sigmoid(gb_ref[...]) * pb).astype(o_ref.dtype)


def _merge(ya, yb, proj, wa, wb, off_ga, off_gb, tm, tn):
    m = ya.shape[0]
    n = wa.shape[1]
    return pl.pallas_call(
        _merge_kernel,
        grid=(m // tm, n // tn),
        in_specs=[
            pl.BlockSpec((tm, ya.shape[1]), lambda i, j: (i, 0)),
            pl.BlockSpec((tm, yb.shape[1]), lambda i, j: (i, 0)),
            pl.BlockSpec((tm, tn), lambda i, j: (i, off_ga // tn + j)),
            pl.BlockSpec((tm, tn), lambda i, j: (i, off_gb // tn + j)),
            pl.BlockSpec((wa.shape[0], tn), lambda i, j: (0, j)),
            pl.BlockSpec((wb.shape[0], tn), lambda i, j: (0, j)),
        ],
        out_specs=pl.BlockSpec((tm, tn), lambda i, j: (i, j)),
        out_shape=jax.ShapeDtypeStruct((m, n), BF16),
        compiler_params=_cparams(("arbitrary", "arbitrary")),
        name="merge",
    )(ya, yb, proj, proj, wa, wb)


def _outproj_kernel(mg_ref, x_ref, w_ref, o_ref):
    o_ref[...] = x_ref[...] + jnp.dot(mg_ref[...], w_ref[...], preferred_element_type=F32)


def _outproj(merged, x, w, tm, tn):
    m, d = x.shape
    return pl.pallas_call(
        _outproj_kernel,
        grid=(m // tm, d // tn),
        in_specs=[
            pl.BlockSpec((tm, merged.shape[1]), lambda i, j: (i, 0)),
            pl.BlockSpec((tm, tn), lambda i, j: (i, j)),
            pl.BlockSpec((w.shape[0], tn), lambda i, j: (0, j)),
        ],
        out_specs=pl.BlockSpec((tm, tn), lambda i, j: (i, j)),
        out_shape=jax.ShapeDtypeStruct((m, d), F32),
        compiler_params=_cparams(("arbitrary", "arbitrary")),
        name="out_proj",
    )(merged, x, w)


def _ffn_kernel(h_ref, nw_ref, wg_ref, wu_ref, cw_ref, cb_ref, wd_ref, hist_ref, nf_ref,
                y_ref, tail_ref, hn_scr, acc_scr, carry_scr, *, n_seq, tiles_per_seq,
                final_norm):
    i = pl.program_id(0)
    j = pl.program_id(1)
    nj = pl.num_programs(1)
    tm = h_ref.shape[0]
    tf = wg_ref.shape[1]
    ts = tm // n_seq

    @pl.when(j == 0)
    def _():
        hh = h_ref[...]
        ms = jnp.mean(hh * hh, axis=-1, keepdims=True)
        hn_scr[...] = ((hh * lax.rsqrt(ms + EPS)) * nw_ref[...]).astype(BF16)
        acc_scr[...] = jnp.zeros(acc_scr.shape, F32)

    hn = hn_scr[...]
    g = jnp.dot(hn, wg_ref[...], preferred_element_type=F32)
    up = jnp.dot(hn, wu_ref[...], preferred_element_type=F32)

    @pl.when((i % tiles_per_seq) == 0)
    def _():
        carry_scr[j] = hist_ref[...]

    w0 = cw_ref[0:1, :]
    w1 = cw_ref[1:2, :]
    w2 = cw_ref[2:3, :]
    cb = cb_ref[...]
    outs = []
    for s in range(n_seq):
        gs = g[s * ts:(s + 1) * ts]
        prev = carry_scr[j, s]
        conv = gs * w2 + _shift_rows(gs, prev, 1) * w1 + _shift_rows(gs, prev, 2) * w0 + cb
        outs.append(_silu(conv) * up[s * ts:(s + 1) * ts])
        tail = gs[ts - SUBLANES:]
        carry_scr[j, s] = tail
        tail_ref[s] = tail
    act = outs[0] if n_seq == 1 else jnp.concatenate(outs, axis=0)
    acc_scr[...] += jnp.dot(act.astype(BF16), wd_ref[...], preferred_element_type=F32)

    @pl.when(j == nj - 1)
    def _():
        o = h_ref[...] + acc_scr[...]
        if final_norm:
            ms = jnp.mean(o * o, axis=-1, keepdims=True)
            o = (o * lax.rsqrt(ms + EPS)) * nf_ref[...]
        y_ref[...] = o


def _ffn(h, norm_w, wg, wu, conv_w, conv_b, wd, hist8, norm_final, final_norm, t_len, tm, tf):
    m, d = h.shape
    f = wg.shape[1]
    nj = f // tf
    if tm >= t_len:
        n_seq, tiles_per_seq = tm // t_len, 1
    else:
        n_seq, tiles_per_seq = 1, t_len // tm
    n_tiles = m // tm
    seq_blk = lambda i, j: ((i // tiles_per_seq), 0, j)
    y, tails = pl.pallas_call(
        functools.partial(_ffn_kernel, n_seq=n_seq, tiles_per_seq=tiles_per_seq, final_norm=final_norm),
        grid=(n_tiles, nj),
        in_specs=[
            pl.BlockSpec((tm, d), lambda i, j: (i, 0)),
            pl.BlockSpec((1, d), lambda i, j: (0, 0)),
            pl.BlockSpec((d, tf), lambda i, j: (0, j)),
            pl.BlockSpec((d, tf), lambda i, j: (0, j)),
            pl.BlockSpec((FFN_CONV, tf), lambda i, j: (0, j)),
            pl.BlockSpec((1, tf), lambda i, j: (0, j)),
            pl.BlockSpec((tf, d), lambda i, j: (j, 0)),
            pl.BlockSpec((n_seq, SUBLANES, tf), seq_blk),
            pl.BlockSpec((1, d), lambda i, j: (0, 0)),
        ],
        out_specs=[
            pl.BlockSpec((tm, d), lambda i, j: (i, 0)),
            pl.BlockSpec((n_seq, SUBLANES, tf), lambda i, j: (i, 0, j)),
        ],
        out_shape=[jax.ShapeDtypeStruct((m, d), F32),
                   jax.ShapeDtypeStruct((n_tiles * n_seq, SUBLANES, f), F32)],
        scratch_shapes=[pltpu.VMEM((tm, d), BF16), pltpu.VMEM((tm, d), F32),
                        pltpu.VMEM((nj, n_seq, SUBLANES, tf), F32)],
        compiler_params=_cparams(("arbitrary", "arbitrary")),
        name="conv_ffn",
    )(h, norm_w, wg, wu, conv_w, conv_b, wd, hist8, norm_final)
    return y, tails


def _pad_hist(hist):
    b, r, c = hist.shape
    return jnp.concatenate([jnp.zeros((b, SUBLANES - r, c), hist.dtype), hist], axis=1)


def _pick(n, cands):
    for c in cands:
        if n % c == 0:
            return c
    return n


def _layer(x, pos_offset, k_hist, v_hist, gdn_conv_hist, gdn_s0, ffn_hist, lam_init, wts, norm_final,
           final_norm):
    nb, t_len, d_model = x.shape
    m = nb * t_len
    past = k_hist.shape[1]
    x2 = x.reshape(m, d_model)
    qk_dim = GDN_HK * GDN_D
    v_dim = GDN_HV * GDN_D
    dqk = DIFF_H * 2 * DIFF_DK
    off = {"gq": 0, "gk": qk_dim, "gv": 2 * qk_dim, "gz": 2 * qk_dim + v_dim}
    off["dq"] = off["gz"] + v_dim
    off["dk"] = off["dq"] + dqk
    off["dv"] = off["dk"] + dqk
    off["ga"] = off["dv"] + DIFF_H * DIFF_DV
    off["gb"] = off["ga"] + d_model

    tm = _pick(m, (1024, 512, 256))
    tn = _pick(wts["w_main"].shape[1], (1024, 512, 256, 128))
    proj, gates = _inproj(x2, wts["norm_mix"], wts["w_main"], wts["w_gates"], tm, tn)

    tmr = _pick(m, (256,))
    pos = pos_offset + jnp.arange(t_len, dtype=jnp.int32)
    tables = _rope_tables(pos)
    if t_len < tmr:
        tables = [jnp.tile(tb, (tmr // t_len, 1)) for tb in tables]
    n_tab = max(t_len // tmr, 1)
    q_r, k_rows, k_b, v_rows, v_b = _rope(proj, tables, tmr, n_tab, off["dq"], off["dk"], off["dv"], dqk)

    chunk = CHUNK if t_len % CHUNK == 0 else t_len
    tt = _pick(t_len, (256,)) if t_len % CHUNK == 0 else t_len
    kg = _pick(GDN_HK, (4, 2))
    ya, gdn_s = _gdn(proj, gates, _pad_hist(gdn_conv_hist), wts["gdn_conv_w"], wts["alog"], wts["dtb"],
                     wts["gdn_norm_w"], gdn_s0, nb, t_len, tt, chunk, kg)
    qkv_tail = proj.reshape(nb, t_len, -1)[:, t_len - (GDN_CONV - 1):, :off["gz"]]
    if t_len >= GDN_CONV - 1:
        gdn_conv_new = qkv_tail
    else:
        gdn_conv_new = jnp.concatenate([gdn_conv_hist, proj.reshape(nb, t_len, -1)[:, :, :off["gz"]]],
                                       axis=1)[:, -(GDN_CONV - 1):]

    s_len = past + t_len
    if past:
        k_all = jnp.concatenate([k_hist.reshape(nb, past, dqk).astype(BF16), k_b.reshape(nb, t_len, dqk)],
                                axis=1).reshape(nb * s_len, dqk)
        v_all = jnp.concatenate([v_hist.reshape(nb, past, dqk).astype(BF16), v_b.reshape(nb, t_len, dqk)],
                                axis=1).reshape(nb * s_len, dqk)
    else:
        k_all, v_all = k_b, v_b
    tq = _pick(t_len, (512, 256))
    tqs = _pick(tq, (256,))
    tk = _pick(s_len, (512, 256))
    yb = _attn(q_r, k_all, v_all, wts["lam4"], wts["subln_w"], nb, t_len, s_len, tq, tqs, tk,
               pos_offset, lam_init)

    tmm = _pick(m, (512, 256))
    tnm = _pick(d_model, (512, 256, 128))
    merged = _merge(ya, yb, proj, wts["w_br_a"], wts["w_br_b"], off["ga"], off["gb"], tmm, tnm)
    h = _outproj(merged, x2, wts["w_out"], tmm, tnm)
    tmf = _pick(m, (512, 256))
    y, tails = _ffn(h, wts["norm_ffn"], wts["w_gate"], wts["w_up"], wts["ffn_conv_w"], wts["ffn_conv_b"],
                    wts["w_down"], _pad_hist(ffn_hist), norm_final, final_norm, t_len, tmf, 512)
    d_ff = tails.shape[-1]
    if tmf >= t_len:
        seq_tails = tails
    else:
        seq_tails = tails.reshape(nb, t_len // tmf, SUBLANES, d_ff)[:, -1]
    ffn_new = seq_tails[:, SUBLANES - (FFN_CONV - 1):]

    return (y.reshape(nb, t_len, d_model),
            k_rows.reshape(nb, t_len, DIFF_H, 2 * DIFF_DK),
            v_rows.reshape(nb, t_len, DIFF_H, DIFF_DV),
            gdn_conv_new, gdn_s, ffn_new)


def _prep_weights(l, norm_mix, w_in, gdn_conv_w, gdn_a_log, gdn_dt_bias, gdn_norm_w,
                  lq1, lk1, lq2, lk2, subln_w, w_branch, w_out, norm_ffn,
                  w_gate, w_up, ffn_conv_w, ffn_conv_b, w_down):
    qk_dim = GDN_HK * GDN_D
    v_dim = GDN_HV * GDN_D
    g0 = 2 * qk_dim + 2 * v_dim
    g1 = g0 + 2 * GDN_HV
    wi = w_in[l]
    d_model = wi.shape[0]
    w_main = jnp.concatenate([wi[:, :g0], wi[:, g1:]], axis=1).astype(BF16)
    w_gates = jnp.concatenate([wi[:, g0:g1], jnp.zeros((d_model, LANES - 2 * GDN_HV), wi.dtype)],
                              axis=1).astype(BF16)
    pad = jnp.zeros((LANES - GDN_HV,), F32)
    return {
        "norm_mix": norm_mix[l].reshape(1, -1),
        "w_main": w_main,
        "w_gates": w_gates,
        "gdn_conv_w": gdn_conv_w[l],
        "alog": jnp.concatenate([gdn_a_log[l].astype(F32), pad]).reshape(1, LANES),
        "dtb": jnp.concatenate([gdn_dt_bias[l].astype(F32), pad]).reshape(1, LANES),
        "gdn_norm_w": gdn_norm_w[l].reshape(1, -1),
        "lam4": jnp.stack([lq1[l], lk1[l], lq2[l], lk2[l]], axis=0).astype(F32),
        "subln_w": subln_w[l].reshape(1, -1),
        "w_br_a": w_branch[l][:v_dim].astype(BF16),
        "w_br_b": w_branch[l][v_dim:].astype(BF16),
        "w_out": w_out[l].astype(BF16),
        "norm_ffn": norm_ffn[l].reshape(1, -1),
        "w_gate": w_gate[l].astype(BF16),
        "w_up": w_up[l].astype(BF16),
        "ffn_conv_w": ffn_conv_w[l],
        "ffn_conv_b": ffn_conv_b[l].reshape(1, -1),
        "w_down": w_down[l].astype(BF16),
    }


def kernel(x_prompt, x_sample, cache_diff_k, cache_diff_v, state_gdn_conv, state_gdn_rec, state_ffn_conv,
           norm_mix, w_in, gdn_conv_w, gdn_a_log, gdn_dt_bias, gdn_norm_w,
           diff_lambda_q1, diff_lambda_k1, diff_lambda_q2, diff_lambda_k2, diff_subln_w,
           w_branch, w_out, norm_ffn, ffn_w_gate, ffn_w_up, ffn_conv_w, ffn_conv_b, ffn_w_down,
           norm_final):
    depth = w_in.shape[0]
    bp = x_prompt.shape[0]
    past = cache_diff_k.shape[2]
    dt = x_prompt.dtype
    d_ff = ffn_w_gate.shape[-1]
    conv_dim = gdn_conv_w.shape[-1]
    nf = norm_final.reshape(1, -1)
    hp, hs = x_prompt, x_sample
    p_out = [[], [], [], [], []]
    s_out = [[], [], [], [], []]
    for l in range(depth):
        lam_init = 0.8 - 0.6 * math.exp(-0.3 * l)
        wts = _prep_weights(l, norm_mix, w_in, gdn_conv_w, gdn_a_log, gdn_dt_bias, gdn_norm_w,
                            diff_lambda_q1, diff_lambda_k1, diff_lambda_q2, diff_lambda_k2, diff_subln_w,
                            w_branch, w_out, norm_ffn, ffn_w_gate, ffn_w_up, ffn_conv_w, ffn_conv_b,
                            ffn_w_down)
        last = l == depth - 1
        hp, *sp = _layer(hp, 0,
                         jnp.zeros((bp, 0, DIFF_H, 2 * DIFF_DK), dt),
                         jnp.zeros((bp, 0, DIFF_H, DIFF_DV), dt),
                         jnp.zeros((bp, GDN_CONV - 1, conv_dim), dt),
                         jnp.zeros((bp, GDN_HV, GDN_D, GDN_D), dt),
                         jnp.zeros((bp, FFN_CONV - 1, d_ff), dt),
                         lam_init, wts, nf, last)
        hs, *ss = _layer(hs, past, cache_diff_k[l], cache_diff_v[l], state_gdn_conv[l], state_gdn_rec[l],
                         state_ffn_conv[l], lam_init, wts, nf, last)
        for i in range(5):
            p_out[i].append(sp[i])
            s_out[i].append(ss[i])
    p_k, p_v, p_gdn_conv, p_gdn_rec, p_ffn_conv = [jnp.stack(a, axis=0) for a in p_out]
    s_k, s_v, s_gdn_conv, s_gdn_rec, s_ffn_conv = [jnp.stack(a, axis=0) for a in s_out]
    return (hp, hs, p_k, p_v, p_gdn_conv, p_gdn_rec, p_ffn_conv,
            s_k, s_v, s_gdn_conv, s_gdn_rec, s_ffn_conv)
```

```python
import functools
import math

import jax
import jax.numpy as jnp
from jax import lax
from jax.experimental import pallas as pl
from jax.experimental.pallas import tpu as pltpu

F32 = jnp.float32
BF16 = jnp.bfloat16

CHUNK = 64
EPS = 1e-6
GDN_HK = 16
GDN_HV = 32
GDN_D = 128
GDN_CONV = 4
DIFF_H = 8
DIFF_DK = 128
DIFF_DV = 2 * DIFF_DK
ROPE_THETA = 500000.0
ROT_DIM = DIFF_DK // 4
FFN_CONV = 3
LANES = 128
SUBLANES = 8
NEG_BIG = -1e30
VMEM_LIMIT = 56 * 1024 * 1024


def _cparams(sem):
    return pltpu.CompilerParams(dimension_semantics=sem, vmem_limit_bytes=VMEM_LIMIT)


def _sigmoid(x):
    return 1.0 / (1.0 + jnp.exp(-x))


def _silu(x):
    return x * _sigmoid(x)


def _bdot(a, b):
    return jnp.dot(a.astype(BF16), b.astype(BF16), preferred_element_type=F32)


def _bdot_nt(a, b):
    return lax.dot_general(a.astype(BF16), b.astype(BF16), (((1,), (1,)), ((), ())),
                           preferred_element_type=F32)


def _shift_rows(x, prev8, s):
    n = x.shape[0]
    xr = pltpu.roll(x, s, axis=0)
    pr = pltpu.roll(prev8, s, axis=0)
    row = lax.broadcasted_iota(jnp.int32, pr.shape, 0)
    head = jnp.where(row < s, pr, xr[:SUBLANES])
    if n == SUBLANES:
        return head
    return jnp.concatenate([head, xr[SUBLANES:]], axis=0)


def _inproj_kernel(x_ref, nw_ref, w_ref, wg_ref, o_ref, og_ref, xn_ref):
    @pl.when(pl.program_id(1) == 0)
    def _():
        x = x_ref[...]
        ms = jnp.mean(x * x, axis=-1, keepdims=True)
        xn = ((x * lax.rsqrt(ms + EPS)) * nw_ref[...]).astype(BF16)
        xn_ref[...] = xn
        og_ref[...] = jnp.dot(xn, wg_ref[...], preferred_element_type=F32)

    o_ref[...] = jnp.dot(xn_ref[...], w_ref[...], preferred_element_type=F32)


def _inproj(x, norm_w, w_main, w_gates, tm, tn):
    m, d = x.shape
    n = w_main.shape[1]
    return pl.pallas_call(
        _inproj_kernel,
        grid=(m // tm, n // tn),
        in_specs=[
            pl.BlockSpec((tm, d), lambda i, j: (i, 0)),
            pl.BlockSpec((1, d), lambda i, j: (0, 0)),
            pl.BlockSpec((d, tn), lambda i, j: (0, j)),
            pl.BlockSpec((d, LANES), lambda i, j: (0, 0)),
        ],
        out_specs=[
            pl.BlockSpec((tm, tn), lambda i, j: (i, j)),
            pl.BlockSpec((tm, LANES), lambda i, j: (i, 0)),
        ],
        out_shape=[jax.ShapeDtypeStruct((m, n), F32), jax.ShapeDtypeStruct((m, LANES), F32)],
        scratch_shapes=[pltpu.VMEM((tm, d), BF16)],
        compiler_params=_cparams(("arbitrary", "arbitrary")),
        name="in_proj",
    )(x, norm_w, w_main, w_gates)


def _rope_kernel(dq_ref, dk_ref, dv_ref, c_ref, s1_ref, s2_ref,
                 q_ref, kf_ref, kb_ref, vf_ref, vb_ref, *, scale):
    c = c_ref[...]
    s1 = s1_ref[...]
    s2 = s2_ref[...]
    half = ROT_DIM // 2
    for g in range(dq_ref.shape[1] // LANES):
        sl = slice(g * LANES, (g + 1) * LANES)
        xq = dq_ref[:, sl]
        rq = xq * c + pltpu.roll(xq, LANES - half, axis=1) * s1 + pltpu.roll(xq, half, axis=1) * s2
        q_ref[:, sl] = (rq * scale).astype(BF16)
        xk = dk_ref[:, sl]
        rk = xk * c + pltpu.roll(xk, LANES - half, axis=1) * s1 + pltpu.roll(xk, half, axis=1) * s2
        kf_ref[:, sl] = rk
        kb_ref[:, sl] = rk.astype(BF16)
    xv = dv_ref[...]
    vf_ref[...] = xv
    vb_ref[...] = xv.astype(BF16)


def _rope_tables(pos):
    half = ROT_DIM // 2
    inv = jnp.exp(-math.log(ROPE_THETA) * jnp.arange(0, ROT_DIM, 2, dtype=F32) / ROT_DIM)
    ang = pos.astype(F32)[:, None] * inv[None, :]
    cos, sin = jnp.cos(ang), jnp.sin(ang)
    n = pos.shape[0]
    ones = jnp.ones((n, LANES - ROT_DIM), F32)
    zeros = jnp.zeros((n, LANES - half), F32)
    c = jnp.concatenate([cos, cos, ones], axis=1)
    s1 = jnp.concatenate([-sin, zeros], axis=1)
    s2 = jnp.concatenate([jnp.zeros((n, half), F32), sin, jnp.zeros((n, LANES - ROT_DIM), F32)], axis=1)
    return c, s1, s2


def _rope(proj, tables, tm, n_tab_blocks, off_q, off_k, off_v, width):
    m = proj.shape[0]
    wb = width
    row = lambda i: (i, 0)
    tab = lambda i: (i % n_tab_blocks, 0)
    outs = pl.pallas_call(
        functools.partial(_rope_kernel, scale=DIFF_DK ** -0.5 * math.log2(math.e)),
        grid=(m // tm,),
        in_specs=[
            pl.BlockSpec((tm, wb), lambda i: (i, off_q // wb)),
            pl.BlockSpec((tm, wb), lambda i: (i, off_k // wb)),
            pl.BlockSpec((tm, wb), lambda i: (i, off_v // wb)),
            pl.BlockSpec((tm, LANES), tab),
            pl.BlockSpec((tm, LANES), tab),
            pl.BlockSpec((tm, LANES), tab),
        ],
        out_specs=[pl.BlockSpec((tm, wb), row)] * 5,
        out_shape=[jax.ShapeDtypeStruct((m, wb), BF16), jax.ShapeDtypeStruct((m, wb), F32),
                   jax.ShapeDtypeStruct((m, wb), BF16), jax.ShapeDtypeStruct((m, wb), F32),
                   jax.ShapeDtypeStruct((m, wb), BF16)],
        compiler_params=_cparams(("arbitrary",)),
        name="rope",
    )(proj, proj, proj, *tables)
    return outs


def _split2(x):
    hi = x.astype(BF16)
    lo = (x - hi.astype(F32)).astype(BF16)
    return hi, lo


def _gdn_kernel(q_ref, k_ref, v_ref, z_ref, g_ref, hq_ref, hk_ref, hv_ref,
                cwq_ref, cwk_ref, cwv_ref, alog_ref, dtb_ref, nw_ref, s0_ref,
                o_ref, sfin_ref, s_scr, pq_scr, pk_scr, pv_scr, *, chunk, kg):
    h = pl.program_id(1)
    t = pl.program_id(2)
    nt = pl.num_programs(2)
    tt = q_ref.shape[0]
    d = GDN_D
    c = chunk

    @pl.when(t == 0)
    def _():
        s_scr[...] = s0_ref[0]
        pq_scr[...] = hq_ref[0]
        pk_scr[...] = hk_ref[0]
        pv_scr[...] = hv_ref[0]

    def conv_silu(x, prev, w_ref):
        acc = x * w_ref[GDN_CONV - 1:GDN_CONV, :]
        for s in range(1, GDN_CONV):
            acc = acc + _shift_rows(x, prev, s) * w_ref[GDN_CONV - 1 - s:GDN_CONV - s, :]
        return _silu(acc)

    xq = q_ref[...]
    xk = k_ref[...]
    xv = v_ref[...]
    qc = conv_silu(xq, pq_scr[...], cwq_ref)
    kc = conv_silu(xk, pk_scr[...], cwk_ref)
    v = conv_silu(xv, pv_scr[...], cwv_ref)
    pq_scr[...] = xq[tt - SUBLANES:]
    pk_scr[...] = xk[tt - SUBLANES:]
    pv_scr[...] = xv[tt - SUBLANES:]

    q, k = [], []
    for kh in range(kg):
        qh = qc[:, kh * d:(kh + 1) * d]
        kh_ = kc[:, kh * d:(kh + 1) * d]
        q.append(qh * lax.rsqrt(jnp.sum(qh * qh, axis=-1, keepdims=True) + EPS) * (d ** -0.5))
        k.append(kh_ * lax.rsqrt(jnp.sum(kh_ * kh_, axis=-1, keepdims=True) + EPS))

    gates = g_ref[...]
    xg = gates + dtb_ref[...]
    softplus = jnp.maximum(xg, 0.0) + jnp.log(1.0 + jnp.exp(-jnp.abs(xg)))
    g_all = -jnp.exp(alog_ref[...]) * softplus
    beta_all = _sigmoid(gates)
    lane = lax.broadcasted_iota(jnp.int32, gates.shape, 1)

    ri = lax.broadcasted_iota(jnp.int32, (c, c), 0)
    ci = lax.broadcasted_iota(jnp.int32, (c, c), 1)
    causal = ri >= ci
    strict = ri > ci
    tril_b = causal.astype(BF16)
    eye = (ri == ci).astype(F32)
    rm = lax.broadcasted_iota(jnp.int32, (c, 2 * LANES), 0)
    cm = lax.broadcasted_iota(jnp.int32, (c, 2 * LANES), 1)
    cum_mask = jnp.where(((cm < c) & (rm > cm)) | (cm >= LANES), 1.0, 0.0)

    z = z_ref[...]
    nw = nw_ref[...]
    n_sq = int(math.log2(c)) - 1
    n_chunks = tt // c
    heads = range(2 * kg)
    items = [(e, n) for e in heads for n in range(n_chunks)]
    rows = [slice(n * c, (n + 1) * c) for n in range(n_chunks)]

    g_cols, b_cols = [], []
    for e in heads:
        hv = 2 * kg * h + e
        g_cols.append(jnp.sum(jnp.where(lane == hv, g_all, 0.0), axis=-1, keepdims=True))
        b_cols.append(jnp.sum(jnp.where(lane == GDN_HV + hv, beta_all, 0.0), axis=-1, keepdims=True))

    gd = {}
    for e, n in items:
        gc = jnp.broadcast_to(g_cols[e][rows[n]], (c, 2 * LANES)) * cum_mask
        g_hi, g_lo = _split2(gc)
        gd[e, n] = (jnp.dot(tril_b, g_hi, preferred_element_type=F32)
                    + jnp.dot(tril_b, g_lo, preferred_element_type=F32))
    kk = {(kh, n): _bdot_nt(k[kh][r], k[kh][r]) for kh in range(kg) for n, r in enumerate(rows)}
    qk = {(kh, n): _bdot_nt(q[kh][r], k[kh][r]) for kh in range(kg) for n, r in enumerate(rows)}

    x, tinv, rhs, intra, qg, kdt, exp_gl = {}, {}, {}, {}, {}, {}, {}
    for e, n in items:
        r = rows[n]
        kh = e // 2
        gb = gd[e, n][:, LANES:]
        decay = jnp.where(causal, jnp.exp(gd[e, n][:, :c]), 0.0)
        exp_g = jnp.exp(gb)
        g_last = gb[c - 1:c, :]
        exp_gl[e, n] = jnp.exp(g_last)
        bb = jnp.broadcast_to(b_cols[e][r], (c, LANES))
        a = jnp.where(strict, -(kk[kh, n] * bb[:, :c] * decay), 0.0)
        x[e, n] = a
        tinv[e, n] = eye + a
        kb = k[kh][r] * bb
        rhs[e, n] = jnp.concatenate([v[r, e * d:(e + 1) * d] * bb, kb * exp_g], axis=1)
        intra[e, n] = jnp.where(causal, qk[kh, n] * decay, 0.0)
        qg[e, n] = q[kh][r] * exp_g
        kdt[e, n] = (k[kh][r] * jnp.exp(g_last - gb)).T

    for it in items:
        x[it] = _bdot(x[it], x[it])
    for _ in range(n_sq - 1):
        for it in items:
            both = _bdot(jnp.concatenate([x[it], tinv[it]], axis=0), x[it])
            x[it] = both[:c]
            tinv[it] = tinv[it] + both[c:]
    for it in items:
        tinv[it] = tinv[it] + _bdot(tinv[it], x[it])

    u, w, kw, ku = {}, {}, {}, {}
    for it in items:
        uw = _bdot(tinv[it], rhs[it])
        u[it], w[it] = uw[:, :d], uw[:, d:]
    for it in items:
        kwu = _bdot(kdt[it], jnp.concatenate([w[it], u[it]], axis=1))
        kw[it], ku[it] = kwu[:, :d], kwu[:, d:]

    s_in = {}
    s_cur = [s_scr[e] for e in heads]
    for n in range(n_chunks):
        for e in heads:
            s_in[e, n] = s_cur[e]
            s_cur[e] = s_cur[e] * exp_gl[e, n] + ku[e, n] - _bdot(kw[e, n], s_cur[e])
    for e in heads:
        s_scr[e] = s_cur[e]

    for it in items:
        e, n = it
        ws = _bdot(jnp.concatenate([w[it], qg[it]], axis=0), s_in[it])
        v_new = u[it] - ws[:c]
        o = ws[c:] + _bdot(intra[it], v_new)
        o = o * lax.rsqrt(jnp.mean(o * o, axis=-1, keepdims=True) + EPS) * nw
        o = o * _silu(z[rows[n], e * d:(e + 1) * d])
        o_ref[rows[n], e * d:(e + 1) * d] = o.astype(o_ref.dtype)

    @pl.when(t == nt - 1)
    def _():
        sfin_ref[0] = s_scr[...]


def _gdn(proj, gates, hist8, conv_w, alog, dtb, nw, s0, nb, t_len, tt, chunk, kg):
    m = proj.shape[0]
    nt = t_len // tt
    d = GDN_D
    wqk = kg * d
    wv = 2 * kg * d
    nhb = GDN_HK // kg
    row = lambda b, h, t: b * nt + t
    ya, sfin = pl.pallas_call(
        functools.partial(_gdn_kernel, chunk=chunk, kg=kg),
        grid=(nb, nhb, nt),
        in_specs=[
            pl.BlockSpec((tt, wqk), lambda b, h, t: (row(b, h, t), h)),
            pl.BlockSpec((tt, wqk), lambda b, h, t: (row(b, h, t), nhb + h)),
            pl.BlockSpec((tt, wv), lambda b, h, t: (row(b, h, t), nhb + h)),
            pl.BlockSpec((tt, wv), lambda b, h, t: (row(b, h, t), 2 * nhb + h)),
            pl.BlockSpec((tt, LANES), lambda b, h, t: (row(b, h, t), 0)),
            pl.BlockSpec((1, SUBLANES, wqk), lambda b, h, t: (b, 0, h)),
            pl.BlockSpec((1, SUBLANES, wqk), lambda b, h, t: (b, 0, nhb + h)),
            pl.BlockSpec((1, SUBLANES, wv), lambda b, h, t: (b, 0, nhb + h)),
            pl.BlockSpec((GDN_CONV, wqk), lambda b, h, t: (0, h)),
            pl.BlockSpec((GDN_CONV, wqk), lambda b, h, t: (0, nhb + h)),
            pl.BlockSpec((GDN_CONV, wv), lambda b, h, t: (0, nhb + h)),
            pl.BlockSpec((1, LANES), lambda b, h, t: (0, 0)),
            pl.BlockSpec((1, LANES), lambda b, h, t: (0, 0)),
            pl.BlockSpec((1, d), lambda b, h, t: (0, 0)),
            pl.BlockSpec((1, 2 * kg, d, d), lambda b, h, t: (b, h, 0, 0)),
        ],
        out_specs=[
            pl.BlockSpec((tt, wv), lambda b, h, t: (row(b, h, t), h)),
            pl.BlockSpec((1, 2 * kg, d, d), lambda b, h, t: (b, h, 0, 0)),
        ],
        out_shape=[jax.ShapeDtypeStruct((m, GDN_HV * d), BF16),
                   jax.ShapeDtypeStruct((nb, GDN_HV, d, d), F32)],
        scratch_shapes=[pltpu.VMEM((2 * kg, d, d), F32), pltpu.VMEM((SUBLANES, wqk), F32),
                        pltpu.VMEM((SUBLANES, wqk), F32), pltpu.VMEM((SUBLANES, wv), F32)],
        compiler_params=_cparams(("arbitrary", "arbitrary", "arbitrary")),
        name="gdn",
    )(proj, proj, proj, proj, gates, hist8, hist8, hist8, conv_w, conv_w, conv_w,
      alog, dtb, nw, s0)
    return ya, sfin


def _attn_kernel(q_ref, k_ref, v_ref, lam_ref, sw_ref, o_ref, acc_ref, m_ref, l_ref,
                 *, tk, tqs, q_off, lam_init):
    i = pl.program_id(2)
    tq = q_ref.shape[0]
    s_len = k_ref.shape[0]
    dk = DIFF_DK
    dv = DIFF_DV
    shift = int(math.log2(CHUNK))
    q0 = q_off + i * tq
    lo_vis = jnp.minimum(((q0 >> shift) + 1) * CHUNK, s_len)
    hi_vis = jnp.minimum((((q0 + tq - 1) >> shift) + 1) * CHUNK, s_len)
    n_full = lo_vis // tk
    n_tot = (hi_vis + tk - 1) // tk
    chains = [(r, c) for r in range(tq // tqs) for c in range(2)]
    lane_tiles = tk // LANES if tk % LANES == 0 else 0

    m_ref[...] = jnp.full(m_ref.shape, NEG_BIG, F32)
    l_ref[...] = jnp.zeros(l_ref.shape, F32)
    acc_ref[...] = jnp.zeros(acc_ref.shape, F32)

    def step(j, masked):
        k0 = pl.multiple_of(j * tk, tk)
        ks = k_ref[pl.ds(k0, tk), :]
        vs = v_ref[pl.ds(k0, tk), :]
        s, m_prev, m_new, p = {}, {}, {}, {}
        for ch in chains:
            r, c = ch
            s[ch] = lax.dot_general(q_ref[r * tqs:(r + 1) * tqs, c * dk:(c + 1) * dk],
                                    ks[:, c * dk:(c + 1) * dk],
                                    (((1,), (1,)), ((), ())), preferred_element_type=F32)
        if masked:
            for ch in chains:
                r, c = ch
                qpos = q0 + r * tqs + lax.broadcasted_iota(jnp.int32, s[ch].shape, 0)
                kpos = k0 + lax.broadcasted_iota(jnp.int32, s[ch].shape, 1)
                s[ch] = jnp.where((kpos >> shift) <= (qpos >> shift), s[ch], NEG_BIG)
        for n, ch in enumerate(chains):
            m_prev[ch] = m_ref[n]
            m_new[ch] = jnp.maximum(m_prev[ch], jnp.max(s[ch], axis=-1, keepdims=True))
        for n, ch in enumerate(chains):
            if lane_tiles:
                m_t = jnp.concatenate([m_new[ch]] * lane_tiles, axis=1)
            else:
                m_t = m_new[ch][:, :1]
            p[ch] = jnp.exp2(s[ch] - m_t)
        for n, ch in enumerate(chains):
            alpha = jnp.exp2(m_prev[ch] - m_new[ch])
            l_ref[n] = alpha * l_ref[n] + jnp.sum(p[ch], axis=-1, keepdims=True)
            pv = jnp.dot(p[ch].astype(BF16), vs, preferred_element_type=F32)
            acc_ref[n] = jnp.concatenate([alpha] * (dv // LANES), axis=1) * acc_ref[n] + pv
            m_ref[n] = m_new[ch]

    def body_full(j, carry):
        step(j, False)
        return carry

    def body_masked(j, carry):
        step(j, True)
        return carry

    lax.fori_loop(0, n_full, body_full, 0)
    lax.fori_loop(n_full, n_tot, body_masked, 0)

    lam = (jnp.exp(jnp.sum(lam_ref[0:1, :] * lam_ref[1:2, :], axis=-1, keepdims=True))
           - jnp.exp(jnp.sum(lam_ref[2:3, :] * lam_ref[3:4, :], axis=-1, keepdims=True)) + lam_init)
    for r in range(tq // tqs):
        l0 = jnp.concatenate([l_ref[2 * r]] * (dv // LANES), axis=1)
        l1 = jnp.concatenate([l_ref[2 * r + 1]] * (dv // LANES), axis=1)
        o = acc_ref[2 * r] / l0 - lam * (acc_ref[2 * r + 1] / l1)
        o = (o * lax.rsqrt(jnp.mean(o * o, axis=-1, keepdims=True) + EPS)) * sw_ref[...]
        o_ref[r * tqs:(r + 1) * tqs, :] = (o * (1.0 - lam_init)).astype(o_ref.dtype)


def _attn(q, k, v, lam4, subln_w, nb, t_len, s_len, tq, tqs, tk, q_off, lam_init):
    m = q.shape[0]
    nq = t_len // tq
    dv = DIFF_DV
    n_chains = 2 * (tq // tqs)
    return pl.pallas_call(
        functools.partial(_attn_kernel, tk=tk, tqs=tqs, q_off=q_off, lam_init=lam_init),
        grid=(nb, DIFF_H, nq),
        in_specs=[
            pl.BlockSpec((tq, dv), lambda b, h, i: (b * nq + i, h)),
            pl.BlockSpec((s_len, dv), lambda b, h, i: (b, h)),
            pl.BlockSpec((s_len, dv), lambda b, h, i: (b, h)),
            pl.BlockSpec((4, DIFF_DK), lambda b, h, i: (0, 0)),
            pl.BlockSpec((1, dv), lambda b, h, i: (0, 0)),
        ],
        out_specs=pl.BlockSpec((tq, dv), lambda b, h, i: (b * nq + i, h)),
        out_shape=jax.ShapeDtypeStruct((m, DIFF_H * dv), BF16),
        scratch_shapes=[pltpu.VMEM((n_chains, tqs, dv), F32), pltpu.VMEM((n_chains, tqs, LANES), F32),
                        pltpu.VMEM((n_chains, tqs, LANES), F32)],
        compiler_params=_cparams(("arbitrary", "arbitrary", "arbitrary")),
        name="diff_attn",
    )(q, k, v, lam4, subln_w)


def _attn_cached_kernel(q_ref, kn_ref, vn_ref, kc_ref, vc_ref, lam_ref, sw_ref, o_ref,
                        *, q_off, lam_init, n_heads):
    t = q_ref.shape[0]
    rows = kc_ref.shape[0]
    past = rows // n_heads
    dk = DIFF_DK
    dv = DIFF_DV
    shift = int(math.log2(CHUNK))
    head_shift = int(math.log2(n_heads))
    q_chunk = (q_off + lax.broadcasted_iota(jnp.int32, (t, 1), 0)) >> shift
    col = lax.broadcasted_iota(jnp.int32, (t, rows), 1)
    col_head = col & (n_heads - 1)
    vis_pos = ((col >> head_shift) >> shift) <= q_chunk
    vis_n = ((past + lax.broadcasted_iota(jnp.int32, (t, t), 1)) >> shift) <= q_chunk
    lam = (jnp.exp(jnp.sum(lam_ref[0:1, :] * lam_ref[1:2, :], axis=-1, keepdims=True))
           - jnp.exp(jnp.sum(lam_ref[2:3, :] * lam_ref[3:4, :], axis=-1, keepdims=True)) + lam_init)
    nt = (((1,), (1,)), ((), ()))
    kc = kc_ref[...].astype(BF16)
    vc = vc_ref[...].astype(BF16)
    for h in range(n_heads):
        vis_c = vis_pos & (col_head == h)
        hs = slice(h * dv, (h + 1) * dv)
        q = q_ref[:, hs]
        kn = kn_ref[:, hs]
        vn = vn_ref[:, hs]
        outs = []
        for c in range(2):
            cs = slice(c * dk, (c + 1) * dk)
            sc = jnp.where(vis_c, lax.dot_general(q[:, cs], kc[:, cs], nt, preferred_element_type=F32), NEG_BIG)
            sn = jnp.where(vis_n, lax.dot_general(q[:, cs], kn[:, cs], nt, preferred_element_type=F32), NEG_BIG)
            m = jnp.maximum(jnp.max(sc, axis=-1, keepdims=True), jnp.max(sn, axis=-1, keepdims=True))
            pc = jnp.exp2(sc - m)
            pn = jnp.exp2(sn - m)
            l = jnp.sum(pc, axis=-1, keepdims=True) + jnp.sum(pn, axis=-1, keepdims=True)
            acc = (jnp.dot(pc.astype(BF16), vc, preferred_element_type=F32)
                   + jnp.dot(pn.astype(BF16), vn, preferred_element_type=F32))
            outs.append(acc / l)
        o = outs[0] - lam * outs[1]
        o = (o * lax.rsqrt(jnp.mean(o * o, axis=-1, keepdims=True) + EPS)) * sw_ref[...]
        o_ref[:, hs] = (o * (1.0 - lam_init)).astype(o_ref.dtype)


def _attn_cached(q, k_new, v_new, k_cache, v_cache, lam4, subln_w, nb, t_len, q_off, lam_init):
    past, n_heads, dv = k_cache.shape[1:]
    assert n_heads & (n_heads - 1) == 0 and past >= 1
    width = n_heads * dv
    rows = past * n_heads
    seq = lambda b: (b, 0)
    return pl.pallas_call(
        functools.partial(_attn_cached_kernel, q_off=q_off, lam_init=lam_init, n_heads=n_heads),
        grid=(nb,),
        in_specs=[
            pl.BlockSpec((t_len, width), seq),
            pl.BlockSpec((t_len, width), seq),
            pl.BlockSpec((t_len, width), seq),
            pl.BlockSpec((rows, dv), seq),
            pl.BlockSpec((rows, dv), seq),
            pl.BlockSpec((4, DIFF_DK), lambda b: (0, 0)),
            pl.BlockSpec((1, dv), lambda b: (0, 0)),
        ],
        out_specs=pl.BlockSpec((t_len, width), seq),
        out_shape=jax.ShapeDtypeStruct((nb * t_len, width), BF16),
        compiler_params=_cparams(("arbitrary",)),
        name="diff_attn_cached",
    )(q, k_new, v_new, k_cache.reshape(nb * rows, dv), v_cache.reshape(nb * rows, dv), lam4, subln_w)


def _merge_kernel(ya_ref, yb_ref, ga_ref, gb_ref, wa_ref, wb_ref, o_ref):
    pa = jnp.dot(ya_ref[...], wa_ref[...], preferred_element_type=F32)
    pb = jnp.dot(yb_ref[...], wb_ref[...], preferred_element_type=F32)
    o_ref[...] = (_sigmoid(ga_ref[...]) * pa + _sigmoid(gb_ref[...]) * pb).astype(o_ref.dtype)


def _merge(ya, yb, proj, wa, wb, off_ga, off_gb, tm, tn):
    m = ya.shape[0]
    n = wa.shape[1]
    return pl.pallas_call(
        _merge_kernel,
        grid=(m // tm, n // tn),
        in_specs=[
            pl.BlockSpec((tm, ya.shape[1]), lambda i, j: (i, 0)),
            pl.BlockSpec((tm, yb.shape[1]), lambda i, j: (i, 0)),
            pl.BlockSpec((tm, tn), lambda i, j: (i, off_ga // tn + j)),
            pl.BlockSpec((tm, tn), lambda i, j: (i, off_gb // tn + j)),
            pl.BlockSpec((wa.shape[0], tn), lambda i, j: (0, j)),
            pl.BlockSpec((wb.shape[0], tn), lambda i, j: (0, j)),
        ],
        out_specs=pl.BlockSpec((tm, tn), lambda i, j: (i, j)),
        out_shape=jax.ShapeDtypeStruct((m, n), BF16),
        compiler_params=_cparams(("arbitrary", "arbitrary")),
        name="merge",
    )(ya, yb, proj, proj, wa, wb)


def _outproj_kernel(mg_ref, x_ref, w_ref, o_ref):
    o_ref[...] = x_ref[...] + jnp.dot(mg_ref[...], w_ref[...], preferred_element_type=F32)


def _outproj(merged, x, w, tm, tn):
    m, d = x.shape
    return pl.pallas_call(
        _outproj_kernel,
        grid=(m // tm, d // tn),
        in_specs=[
            pl.BlockSpec((tm, merged.shape[1]), lambda i, j: (i, 0)),
            pl.BlockSpec((tm, tn), lambda i, j: (i, j)),
            pl.BlockSpec((w.shape[0], tn), lambda i, j: (0, j)),
        ],
        out_specs=pl.BlockSpec((tm, tn), lambda i, j: (i, j)),
        out_shape=jax.ShapeDtypeStruct((m, d), F32),
        compiler_params=_cparams(("arbitrary", "arbitrary")),
        name="out_proj",
    )(merged, x, w)


def _ffn_kernel(h_ref, nw_ref, wg_ref, wu_ref, cw_ref, cb_ref, wd_ref, hist_ref, nf_ref,
                y_ref, tail_ref, hn_scr, act_scr, carry_scr, *, n_seq, tiles_per_seq,
                final_norm):
    i = pl.program_id(0)
    j = pl.program_id(1)
    nf = act_scr.shape[0]
    tm = h_ref.shape[0]
    tn = wd_ref.shape[1]
    n_out = h_ref.shape[1] // tn
    ts = tm // n_seq

    @pl.when(j == 0)
    def _():
        hh = h_ref[...]
        ms = jnp.mean(hh * hh, axis=-1, keepdims=True)
        hn_scr[...] = ((hh * lax.rsqrt(ms + EPS)) * nw_ref[...]).astype(BF16)

    @pl.when(j < nf)
    def _():
        hn = hn_scr[...]
        g = jnp.dot(hn, wg_ref[...], preferred_element_type=F32)
        up = jnp.dot(hn, wu_ref[...], preferred_element_type=F32)

        @pl.when((i % tiles_per_seq) == 0)
        def _():
            carry_scr[j] = hist_ref[...]

        w0 = cw_ref[0:1, :]
        w1 = cw_ref[1:2, :]
        w2 = cw_ref[2:3, :]
        cb = cb_ref[...]
        outs = []
        for s in range(n_seq):
            gs = g[s * ts:(s + 1) * ts]
            prev = carry_scr[j, s]
            conv = gs * w2 + _shift_rows(gs, prev, 1) * w1 + _shift_rows(gs, prev, 2) * w0 + cb
            outs.append(_silu(conv) * up[s * ts:(s + 1) * ts])
            tail = gs[ts - SUBLANES:]
            carry_scr[j, s] = tail
            tail_ref[s] = tail
        act = outs[0] if n_seq == 1 else jnp.concatenate(outs, axis=0)
        act_scr[j] = act.astype(BF16)

    for n in range(n_out):
        @pl.when(j == nf + n)
        def _(n=n):
            act_full = jnp.concatenate([act_scr[kf] for kf in range(nf)], axis=1)
            cols = slice(n * tn, (n + 1) * tn)
            y_ref[:, cols] = h_ref[:, cols] + jnp.dot(act_full, wd_ref[...], preferred_element_type=F32)

    if final_norm:
        @pl.when(j == nf + n_out - 1)
        def _():
            o = y_ref[...]
            ms = jnp.mean(o * o, axis=-1, keepdims=True)
            y_ref[...] = (o * lax.rsqrt(ms + EPS)) * nf_ref[...]


def _ffn(h, norm_w, wg, wu, conv_w, conv_b, wd, hist8, norm_final, final_norm, t_len, tm, tf, tn):
    m, d = h.shape
    f = wg.shape[1]
    nf = f // tf
    n_out = d // tn
    if tm >= t_len:
        n_seq, tiles_per_seq = tm // t_len, 1
    else:
        n_seq, tiles_per_seq = 1, t_len // tm
    n_tiles = m // tm
    fj = lambda j: jnp.minimum(j, nf - 1)
    oj = lambda j: jnp.clip(j - nf, 0, n_out - 1)
    y, tails = pl.pallas_call(
        functools.partial(_ffn_kernel, n_seq=n_seq, tiles_per_seq=tiles_per_seq, final_norm=final_norm),
        grid=(n_tiles, nf + n_out),
        in_specs=[
            pl.BlockSpec((tm, d), lambda i, j: (i, 0)),
            pl.BlockSpec((1, d), lambda i, j: (0, 0)),
            pl.BlockSpec((d, tf), lambda i, j: (0, fj(j))),
            pl.BlockSpec((d, tf), lambda i, j: (0, fj(j))),
            pl.BlockSpec((FFN_CONV, tf), lambda i, j: (0, fj(j))),
            pl.BlockSpec((1, tf), lambda i, j: (0, fj(j))),
            pl.BlockSpec((f, tn), lambda i, j: (0, oj(j))),
            pl.BlockSpec((n_seq, SUBLANES, tf), lambda i, j: (i // tiles_per_seq, 0, fj(j))),
            pl.BlockSpec((1, d), lambda i, j: (0, 0)),
        ],
        out_specs=[
            pl.BlockSpec((tm, d), lambda i, j: (i, 0)),
            pl.BlockSpec((n_seq, SUBLANES, tf), lambda i, j: (i, 0, fj(j))),
        ],
        out_shape=[jax.ShapeDtypeStruct((m, d), F32),
                   jax.ShapeDtypeStruct((n_tiles * n_seq, SUBLANES, f), F32)],
        scratch_shapes=[pltpu.VMEM((tm, d), BF16), pltpu.VMEM((nf, tm, tf), BF16),
                        pltpu.VMEM((nf, n_seq, SUBLANES, tf), F32)],
        compiler_params=_cparams(("arbitrary", "arbitrary")),
        name="conv_ffn",
    )(h, norm_w, wg, wu, conv_w, conv_b, wd, hist8, norm_final)
    return y, tails


def _pad_hist(hist):
    b, r, c = hist.shape
    return jnp.concatenate([jnp.zeros((b, SUBLANES - r, c), hist.dtype), hist], axis=1)


def _pick(n, cands):
    for c in cands:
        if n % c == 0:
            return c
    return n


def _layer(x, pos_offset, k_hist, v_hist, gdn_conv_hist, gdn_s0, ffn_hist, lam_init, wts, norm_final,
           final_norm):
    nb, t_len, d_model = x.shape
    m = nb * t_len
    past = k_hist.shape[1]
    x2 = x.reshape(m, d_model)
    qk_dim = GDN_HK * GDN_D
    v_dim = GDN_HV * GDN_D
    dqk = DIFF_H * 2 * DIFF_DK
    off = {"gq": 0, "gk": qk_dim, "gv": 2 * qk_dim, "gz": 2 * qk_dim + v_dim}
    off["dq"] = off["gz"] + v_dim
    off["dk"] = off["dq"] + dqk
    off["dv"] = off["dk"] + dqk
    off["ga"] = off["dv"] + DIFF_H * DIFF_DV
    off["gb"] = off["ga"] + d_model

    tm = _pick(m, (1024, 512, 256))
    tn = _pick(wts["w_main"].shape[1], (1024, 512, 256, 128))
    proj, gates = _inproj(x2, wts["norm_mix"], wts["w_main"], wts["w_gates"], tm, tn)

    tmr = _pick(m, (256,))
    pos = pos_offset + jnp.arange(t_len, dtype=jnp.int32)
    tables = _rope_tables(pos)
    if t_len < tmr:
        tables = [jnp.tile(tb, (tmr // t_len, 1)) for tb in tables]
    n_tab = max(t_len // tmr, 1)
    q_r, k_rows, k_b, v_rows, v_b = _rope(proj, tables, tmr, n_tab, off["dq"], off["dk"], off["dv"], dqk)

    chunk = CHUNK if t_len % CHUNK == 0 else t_len
    tt = _pick(t_len, (256,)) if t_len % CHUNK == 0 else t_len
    kg = _pick(GDN_HK, (4, 2))
    ya, gdn_s = _gdn(proj, gates, _pad_hist(gdn_conv_hist), wts["gdn_conv_w"], wts["alog"], wts["dtb"],
                     wts["gdn_norm_w"], gdn_s0, nb, t_len, tt, chunk, kg)
    qkv_tail = proj.reshape(nb, t_len, -1)[:, t_len - (GDN_CONV - 1):, :off["gz"]]
    if t_len >= GDN_CONV - 1:
        gdn_conv_new = qkv_tail
    else:
        gdn_conv_new = jnp.concatenate([gdn_conv_hist, proj.reshape(nb, t_len, -1)[:, :, :off["gz"]]],
                                       axis=1)[:, -(GDN_CONV - 1):]

    if past:
        yb = _attn_cached(q_r, k_b, v_b, k_hist, v_hist, wts["lam4"], wts["subln_w"], nb, t_len,
                          pos_offset, lam_init)
    else:
        tq = _pick(t_len, (512, 256))
        tqs = _pick(tq, (256,))
        tk = _pick(t_len, (512, 256))
        yb = _attn(q_r, k_b, v_b, wts["lam4"], wts["subln_w"], nb, t_len, t_len, tq, tqs, tk,
                   pos_offset, lam_init)

    tmm = _pick(m, (512, 256))
    tnm = _pick(d_model, (512, 256, 128))
    merged = _merge(ya, yb, proj, wts["w_br_a"], wts["w_br_b"], off["ga"], off["gb"], tmm, tnm)
    h = _outproj(merged, x2, wts["w_out"], tmm, tnm)
    tmf = _pick(m, (512, 256))
    y, tails = _ffn(h, wts["norm_ffn"], wts["w_gate"], wts["w_up"], wts["ffn_conv_w"], wts["ffn_conv_b"],
                    wts["w_down"], _pad_hist(ffn_hist), norm_final, final_norm, t_len, tmf, 512, tnm)
    d_ff = tails.shape[-1]
    if tmf >= t_len:
        seq_tails = tails
    else:
        seq_tails = tails.reshape(nb, t_len // tmf, SUBLANES, d_ff)[:, -1]
    ffn_new = seq_tails[:, SUBLANES - (FFN_CONV - 1):]

    return (y.reshape(nb, t_len, d_model),
            k_rows.reshape(nb, t_len, DIFF_H, 2 * DIFF_DK),
            v_rows.reshape(nb, t_len, DIFF_H, DIFF_DV),
            gdn_conv_new, gdn_s, ffn_new)


def _prep_weights(l, norm_mix, w_in, gdn_conv_w, gdn_a_log, gdn_dt_bias, gdn_norm_w,
                  lq1, lk1, lq2, lk2, subln_w, w_branch, w_out, norm_ffn,
                  w_gate, w_up, ffn_conv_w, ffn_conv_b, w_down):
    qk_dim = GDN_HK * GDN_D
    v_dim = GDN_HV * GDN_D
    g0 = 2 * qk_dim + 2 * v_dim
    g1 = g0 + 2 * GDN_HV
    wi = w_in[l]
    d_model = wi.shape[0]
    w_main = jnp.concatenate([wi[:, :g0], wi[:, g1:]], axis=1).astype(BF16)
    w_gates = jnp.concatenate([wi[:, g0:g1], jnp.zeros((d_model, LANES - 2 * GDN_HV), wi.dtype)],
                              axis=1).astype(BF16)
    pad = jnp.zeros((LANES - GDN_HV,), F32)
    return {
        "norm_mix": norm_mix[l].reshape(1, -1),
        "w_main": w_main,
        "w_gates": w_gates,
        "gdn_conv_w": gdn_conv_w[l],
        "alog": jnp.concatenate([gdn_a_log[l].astype(F32), pad]).reshape(1, LANES),
        "dtb": jnp.concatenate([gdn_dt_bias[l].astype(F32), pad]).reshape(1, LANES),
        "gdn_norm_w": gdn_norm_w[l].reshape(1, -1),
        "lam4": jnp.stack([lq1[l], lk1[l], lq2[l], lk2[l]], axis=0).astype(F32),
        "subln_w": subln_w[l].reshape(1, -1),
        "w_br_a": w_branch[l][:v_dim].astype(BF16),
        "w_br_b": w_branch[l][v_dim:].astype(BF16),
        "w_out": w_out[l].astype(BF16),
        "norm_ffn": norm_ffn[l].reshape(1, -1),
        "w_gate": w_gate[l].astype(BF16),
        "w_up": w_up[l].astype(BF16),
        "ffn_conv_w": ffn_conv_w[l],
        "ffn_conv_b": ffn_conv_b[l].reshape(1, -1),
        "w_down": w_down[l].astype(BF16),
    }


def kernel(x_prompt, x_sample, cache_diff_k, cache_diff_v, state_gdn_conv, state_gdn_rec, state_ffn_conv,
           norm_mix, w_in, gdn_conv_w, gdn_a_log, gdn_dt_bias, gdn_norm_w,
           diff_lambda_q1, diff_lambda_k1, diff_lambda_q2, diff_lambda_k2, diff_subln_w,
           w_branch, w_out, norm_ffn, ffn_w_gate, ffn_w_up, ffn_conv_w, ffn_conv_b, ffn_w_down,
           norm_final):
    depth = w_in.shape[0]
    bp = x_prompt.shape[0]
    past = cache_diff_k.shape[2]
    dt = x_prompt.dtype
    d_ff = ffn_w_gate.shape[-1]
    conv_dim = gdn_conv_w.shape[-1]
    nf = norm_final.reshape(1, -1)
    hp, hs = x_prompt, x_sample
    p_out = [[], [], [], [], []]
    s_out = [[], [], [], [], []]
    for l in range(depth):
        lam_init = 0.8 - 0.6 * math.exp(-0.3 * l)
        wts = _prep_weights(l, norm_mix, w_in, gdn_conv_w, gdn_a_log, gdn_dt_bias, gdn_norm_w,
                            diff_lambda_q1, diff_lambda_k1, diff_lambda_q2, diff_lambda_k2, diff_subln_w,
                            w_branch, w_out, norm_ffn, ffn_w_gate, ffn_w_up, ffn_conv_w, ffn_conv_b,
                            ffn_w_down)
        last = l == depth - 1
        hp, *sp = _layer(hp, 0,
                         jnp.zeros((bp, 0, DIFF_H, 2 * DIFF_DK), dt),
                         jnp.zeros((bp, 0, DIFF_H, DIFF_DV), dt),
                         jnp.zeros((bp, GDN_CONV - 1, conv_dim), dt),
                         jnp.zeros((bp, GDN_HV, GDN_D, GDN_D), dt),
                         jnp.zeros((bp, FFN_CONV - 1, d_ff), dt),
                         lam_init, wts, nf, last)
        hs, *ss = _layer(hs, past, cache_diff_k[l], cache_diff_v[l], state_gdn_conv[l], state_gdn_rec[l],
                         state_ffn_conv[l], lam_init, wts, nf, last)
        for i in range(5):
            p_out[i].append(sp[i])
            s_out[i].append(ss[i])
    p_k, p_v, p_gdn_conv, p_gdn_rec, p_ffn_conv = [jnp.stack(a, axis=0) for a in p_out]
    s_k, s_v, s_gdn_conv, s_gdn_rec, s_ffn_conv = [jnp.stack(a, axis=0) for a in s_out]
    return (hp, hs, p_k, p_v, p_gdn_conv, p_gdn_rec, p_ffn_conv,
            s_k, s_v, s_gdn_conv, s_gdn_rec, s_ffn_conv)
```

```python
import functools
import math

import jax
import jax.numpy as jnp
from jax import lax
from jax.experimental import pallas as pl
from jax.experimental.pallas import tpu as pltpu

F32 = jnp.float32
BF16 = jnp.bfloat16

CHUNK = 64
EPS = 1e-6
GDN_HK = 16
GDN_HV = 32
GDN_D = 128
GDN_CONV = 4
DIFF_H = 8
DIFF_DK = 128
DIFF_DV = 2 * DIFF_DK
ROPE_THETA = 500000.0
ROT_DIM = DIFF_DK // 4
FFN_CONV = 3
LANES = 128
SUBLANES = 8
NEG_BIG = -1e30
VMEM_LIMIT = 56 * 1024 * 1024


def _cparams(sem):
    return pltpu.CompilerParams(dimension_semantics=sem, vmem_limit_bytes=VMEM_LIMIT)


def _sigmoid(x):
    return 1.0 / (1.0 + jnp.exp(-x))


def _silu(x):
    return x * _sigmoid(x)


def _bdot(a, b):
    return jnp.dot(a.astype(BF16), b.astype(BF16), preferred_element_type=F32)


def _bdot_nt(a, b):
    return lax.dot_general(a.astype(BF16), b.astype(BF16), (((1,), (1,)), ((), ())),
                           preferred_element_type=F32)


def _shift_rows(x, prev8, s):
    n = x.shape[0]
    xr = pltpu.roll(x, s, axis=0)
    pr = pltpu.roll(prev8, s, axis=0)
    row = lax.broadcasted_iota(jnp.int32, pr.shape, 0)
    head = jnp.where(row < s, pr, xr[:SUBLANES])
    if n == SUBLANES:
        return head
    return jnp.concatenate([head, xr[SUBLANES:]], axis=0)


def _inproj_kernel(x_ref, nw_ref, w_ref, wg_ref, o_ref, og_ref, xn_ref):
    @pl.when(pl.program_id(1) == 0)
    def _():
        x = x_ref[...]
        ms = jnp.mean(x * x, axis=-1, keepdims=True)
        xn = ((x * lax.rsqrt(ms + EPS)) * nw_ref[...]).astype(BF16)
        xn_ref[...] = xn
        og_ref[...] = jnp.dot(xn, wg_ref[...], preferred_element_type=F32)

    o_ref[...] = jnp.dot(xn_ref[...], w_ref[...], preferred_element_type=F32)


def _inproj(x, norm_w, w_main, w_gates, tm, tn):
    m, d = x.shape
    n = w_main.shape[1]
    return pl.pallas_call(
        _inproj_kernel,
        grid=(m // tm, n // tn),
        in_specs=[
            pl.BlockSpec((tm, d), lambda i, j: (i, 0)),
            pl.BlockSpec((1, d), lambda i, j: (0, 0)),
            pl.BlockSpec((d, tn), lambda i, j: (0, j)),
            pl.BlockSpec((d, LANES), lambda i, j: (0, 0)),
        ],
        out_specs=[
            pl.BlockSpec((tm, tn), lambda i, j: (i, j)),
            pl.BlockSpec((tm, LANES), lambda i, j: (i, 0)),
        ],
        out_shape=[jax.ShapeDtypeStruct((m, n), F32), jax.ShapeDtypeStruct((m, LANES), F32)],
        scratch_shapes=[pltpu.VMEM((tm, d), BF16)],
        compiler_params=_cparams(("arbitrary", "arbitrary")),
        name="in_proj",
    )(x, norm_w, w_main, w_gates)


def _rope_kernel(dq_ref, dk_ref, dv_ref, c_ref, s1_ref, s2_ref,
                 q_ref, kf_ref, kb_ref, vf_ref, vb_ref, *, scale):
    c = c_ref[...]
    s1 = s1_ref[...]
    s2 = s2_ref[...]
    half = ROT_DIM // 2
    for g in range(dq_ref.shape[1] // LANES):
        sl = slice(g * LANES, (g + 1) * LANES)
        xq = dq_ref[:, sl]
        rq = xq * c + pltpu.roll(xq, LANES - half, axis=1) * s1 + pltpu.roll(xq, half, axis=1) * s2
        q_ref[:, sl] = (rq * scale).astype(BF16)
        xk = dk_ref[:, sl]
        rk = xk * c + pltpu.roll(xk, LANES - half, axis=1) * s1 + pltpu.roll(xk, half, axis=1) * s2
        kf_ref[:, sl] = rk
        kb_ref[:, sl] = rk.astype(BF16)
    xv = dv_ref[...]
    vf_ref[...] = xv
    vb_ref[...] = xv.astype(BF16)


def _rope_tables(pos):
    half = ROT_DIM // 2
    inv = jnp.exp(-math.log(ROPE_THETA) * jnp.arange(0, ROT_DIM, 2, dtype=F32) / ROT_DIM)
    ang = pos.astype(F32)[:, None] * inv[None, :]
    cos, sin = jnp.cos(ang), jnp.sin(ang)
    n = pos.shape[0]
    ones = jnp.ones((n, LANES - ROT_DIM), F32)
    zeros = jnp.zeros((n, LANES - half), F32)
    c = jnp.concatenate([cos, cos, ones], axis=1)
    s1 = jnp.concatenate([-sin, zeros], axis=1)
    s2 = jnp.concatenate([jnp.zeros((n, half), F32), sin, jnp.zeros((n, LANES - ROT_DIM), F32)], axis=1)
    return c, s1, s2


def _rope(proj, tables, tm, n_tab_blocks, off_q, off_k, off_v, width):
    m = proj.shape[0]
    wb = width
    row = lambda i: (i, 0)
    tab = lambda i: (i % n_tab_blocks, 0)
    outs = pl.pallas_call(
        functools.partial(_rope_kernel, scale=DIFF_DK ** -0.5 * math.log2(math.e)),
        grid=(m // tm,),
        in_specs=[
            pl.BlockSpec((tm, wb), lambda i: (i, off_q // wb)),
            pl.BlockSpec((tm, wb), lambda i: (i, off_k // wb)),
            pl.BlockSpec((tm, wb), lambda i: (i, off_v // wb)),
            pl.BlockSpec((tm, LANES), tab),
            pl.BlockSpec((tm, LANES), tab),
            pl.BlockSpec((tm, LANES), tab),
        ],
        out_specs=[pl.BlockSpec((tm, wb), row)] * 5,
        out_shape=[jax.ShapeDtypeStruct((m, wb), BF16), jax.ShapeDtypeStruct((m, wb), F32),
                   jax.ShapeDtypeStruct((m, wb), BF16), jax.ShapeDtypeStruct((m, wb), F32),
                   jax.ShapeDtypeStruct((m, wb), BF16)],
        compiler_params=_cparams(("arbitrary",)),
        name="rope",
    )(proj, proj, proj, *tables)
    return outs


def _split2(x):
    hi = x.astype(BF16)
    lo = (x - hi.astype(F32)).astype(BF16)
    return hi, lo


def _gdn_kernel(q_ref, k_ref, v_ref, z_ref, g_ref, hq_ref, hk_ref, hv_ref,
                cwq_ref, cwk_ref, cwv_ref, alog_ref, dtb_ref, nw_ref, s0_ref,
                o_ref, sfin_ref, s_scr, pq_scr, pk_scr, pv_scr, *, chunk, kg):
    h = pl.program_id(1)
    t = pl.program_id(2)
    nt = pl.num_programs(2)
    tt = q_ref.shape[0]
    d = GDN_D
    c = chunk

    @pl.when(t == 0)
    def _():
        s_scr[...] = s0_ref[0]
        pq_scr[...] = hq_ref[0]
        pk_scr[...] = hk_ref[0]
        pv_scr[...] = hv_ref[0]

    def conv_silu(x, prev, w_ref):
        acc = x * w_ref[GDN_CONV - 1:GDN_CONV, :]
        for s in range(1, GDN_CONV):
            acc = acc + _shift_rows(x, prev, s) * w_ref[GDN_CONV - 1 - s:GDN_CONV - s, :]
        return _silu(acc)

    xq = q_ref[...]
    xk = k_ref[...]
    xv = v_ref[...]
    qc = conv_silu(xq, pq_scr[...], cwq_ref)
    kc = conv_silu(xk, pk_scr[...], cwk_ref)
    v = conv_silu(xv, pv_scr[...], cwv_ref)
    pq_scr[...] = xq[tt - SUBLANES:]
    pk_scr[...] = xk[tt - SUBLANES:]
    pv_scr[...] = xv[tt - SUBLANES:]

    q, k = [], []
    for kh in range(kg):
        qh = qc[:, kh * d:(kh + 1) * d]
        kh_ = kc[:, kh * d:(kh + 1) * d]
        q.append(qh * lax.rsqrt(jnp.sum(qh * qh, axis=-1, keepdims=True) + EPS) * (d ** -0.5))
        k.append(kh_ * lax.rsqrt(jnp.sum(kh_ * kh_, axis=-1, keepdims=True) + EPS))

    gates = g_ref[...]
    xg = gates + dtb_ref[...]
    softplus = jnp.maximum(xg, 0.0) + jnp.log(1.0 + jnp.exp(-jnp.abs(xg)))
    g_all = -jnp.exp(alog_ref[...]) * softplus
    beta_all = _sigmoid(gates)
    lane = lax.broadcasted_iota(jnp.int32, gates.shape, 1)

    ri = lax.broadcasted_iota(jnp.int32, (c, c), 0)
    ci = lax.broadcasted_iota(jnp.int32, (c, c), 1)
    causal = ri >= ci
    strict = ri > ci
    tril_b = causal.astype(BF16)
    eye = (ri == ci).astype(F32)
    rm = lax.broadcasted_iota(jnp.int32, (c, 2 * LANES), 0)
    cm = lax.broadcasted_iota(jnp.int32, (c, 2 * LANES), 1)
    cum_mask = jnp.where(((cm < c) & (rm > cm)) | (cm >= LANES), 1.0, 0.0)

    z = z_ref[...]
    nw = nw_ref[...]
    n_sq = int(math.log2(c)) - 1
    n_chunks = tt // c
    heads = range(2 * kg)
    items = [(e, n) for e in heads for n in range(n_chunks)]
    rows = [slice(n * c, (n + 1) * c) for n in range(n_chunks)]

    g_cols, b_cols = [], []
    for e in heads:
        hv = 2 * kg * h + e
        g_cols.append(jnp.sum(jnp.where(lane == hv, g_all, 0.0), axis=-1, keepdims=True))
        b_cols.append(jnp.sum(jnp.where(lane == GDN_HV + hv, beta_all, 0.0), axis=-1, keepdims=True))

    gd = {}
    for e, n in items:
        gc = jnp.broadcast_to(g_cols[e][rows[n]], (c, 2 * LANES)) * cum_mask
        g_hi, g_lo = _split2(gc)
        gd[e, n] = (jnp.dot(tril_b, g_hi, preferred_element_type=F32)
                    + jnp.dot(tril_b, g_lo, preferred_element_type=F32))
    kk = {(kh, n): _bdot_nt(k[kh][r], k[kh][r]) for kh in range(kg) for n, r in enumerate(rows)}
    qk = {(kh, n): _bdot_nt(q[kh][r], k[kh][r]) for kh in range(kg) for n, r in enumerate(rows)}

    x, tinv, rhs, intra, qg, kdt, exp_gl = {}, {}, {}, {}, {}, {}, {}
    for e, n in items:
        r = rows[n]
        kh = e // 2
        gb = gd[e, n][:, LANES:]
        decay = jnp.where(causal, jnp.exp(gd[e, n][:, :c]), 0.0)
        exp_g = jnp.exp(gb)
        g_last = gb[c - 1:c, :]
        exp_gl[e, n] = jnp.exp(g_last)
        bb = jnp.broadcast_to(b_cols[e][r], (c, LANES))
        a = jnp.where(strict, -(kk[kh, n] * bb[:, :c] * decay), 0.0)
        x[e, n] = a
        tinv[e, n] = eye + a
        kb = k[kh][r] * bb
        rhs[e, n] = jnp.concatenate([v[r, e * d:(e + 1) * d] * bb, kb * exp_g], axis=1)
        intra[e, n] = jnp.where(causal, qk[kh, n] * decay, 0.0)
        qg[e, n] = q[kh][r] * exp_g
        kdt[e, n] = (k[kh][r] * jnp.exp(g_last - gb)).T

    for it in items:
        x[it] = _bdot(x[it], x[it])
    for _ in range(n_sq - 1):
        for it in items:
            both = _bdot(jnp.concatenate([x[it], tinv[it]], axis=0), x[it])
            x[it] = both[:c]
            tinv[it] = tinv[it] + both[c:]
    for it in items:
        tinv[it] = tinv[it] + _bdot(tinv[it], x[it])

    u, w, kw, ku = {}, {}, {}, {}
    for it in items:
        uw = _bdot(tinv[it], rhs[it])
        u[it], w[it] = uw[:, :d], uw[:, d:]
    for it in items:
        kwu = _bdot(kdt[it], jnp.concatenate([w[it], u[it]], axis=1))
        kw[it], ku[it] = kwu[:, :d], kwu[:, d:]

    s_in = {}
    s_cur = [s_scr[e] for e in heads]
    for n in range(n_chunks):
        for e in heads:
            s_in[e, n] = s_cur[e]
            s_cur[e] = s_cur[e] * exp_gl[e, n] + ku[e, n] - _bdot(kw[e, n], s_cur[e])
    for e in heads:
        s_scr[e] = s_cur[e]

    for it in items:
        e, n = it
        ws = _bdot(jnp.concatenate([w[it], qg[it]], axis=0), s_in[it])
        v_new = u[it] - ws[:c]
        o = ws[c:] + _bdot(intra[it], v_new)
        o = o * lax.rsqrt(jnp.mean(o * o, axis=-1, keepdims=True) + EPS) * nw
        o = o * _silu(z[rows[n], e * d:(e + 1) * d])
        o_ref[rows[n], e * d:(e + 1) * d] = o.astype(o_ref.dtype)

    @pl.when(t == nt - 1)
    def _():
        sfin_ref[0] = s_scr[...]


def _gdn(proj, gates, hist8, conv_w, alog, dtb, nw, s0, nb, t_len, tt, chunk, kg):
    m = proj.shape[0]
    nt = t_len // tt
    d = GDN_D
    wqk = kg * d
    wv = 2 * kg * d
    nhb = GDN_HK // kg
    row = lambda b, h, t: b * nt + t
    ya, sfin = pl.pallas_call(
        functools.partial(_gdn_kernel, chunk=chunk, kg=kg),
        grid=(nb, nhb, nt),
        in_specs=[
            pl.BlockSpec((tt, wqk), lambda b, h, t: (row(b, h, t), h)),
            pl.BlockSpec((tt, wqk), lambda b, h, t: (row(b, h, t), nhb + h)),
            pl.BlockSpec((tt, wv), lambda b, h, t: (row(b, h, t), nhb + h)),
            pl.BlockSpec((tt, wv), lambda b, h, t: (row(b, h, t), 2 * nhb + h)),
            pl.BlockSpec((tt, LANES), lambda b, h, t: (row(b, h, t), 0)),
            pl.BlockSpec((1, SUBLANES, wqk), lambda b, h, t: (b, 0, h)),
            pl.BlockSpec((1, SUBLANES, wqk), lambda b, h, t: (b, 0, nhb + h)),
            pl.BlockSpec((1, SUBLANES, wv), lambda b, h, t: (b, 0, nhb + h)),
            pl.BlockSpec((GDN_CONV, wqk), lambda b, h, t: (0, h)),
            pl.BlockSpec((GDN_CONV, wqk), lambda b, h, t: (0, nhb + h)),
            pl.BlockSpec((GDN_CONV, wv), lambda b, h, t: (0, nhb + h)),
            pl.BlockSpec((1, LANES), lambda b, h, t: (0, 0)),
            pl.BlockSpec((1, LANES), lambda b, h, t: (0, 0)),
            pl.BlockSpec((1, d), lambda b, h, t: (0, 0)),
            pl.BlockSpec((1, 2 * kg, d, d), lambda b, h, t: (b, h, 0, 0)),
        ],
        out_specs=[
            pl.BlockSpec((tt, wv), lambda b, h, t: (row(b, h, t), h)),
            pl.BlockSpec((1, 2 * kg, d, d), lambda b, h, t: (b, h, 0, 0)),
        ],
        out_shape=[jax.ShapeDtypeStruct((m, GDN_HV * d), BF16),
                   jax.ShapeDtypeStruct((nb, GDN_HV, d, d), F32)],
        scratch_shapes=[pltpu.VMEM((2 * kg, d, d), F32), pltpu.VMEM((SUBLANES, wqk), F32),
                        pltpu.VMEM((SUBLANES, wqk), F32), pltpu.VMEM((SUBLANES, wv), F32)],
        compiler_params=_cparams(("arbitrary", "arbitrary", "arbitrary")),
        name="gdn",
    )(proj, proj, proj, proj, gates, hist8, hist8, hist8, conv_w, conv_w, conv_w,
      alog, dtb, nw, s0)
    return ya, sfin


def _attn_kernel(q_ref, k_ref, v_ref, lam_ref, sw_ref, o_ref, acc_ref, m_ref, l_ref,
                 *, tk, tqs, q_off, lam_init):
    i = pl.program_id(2)
    tq = q_ref.shape[0]
    s_len = k_ref.shape[0]
    dk = DIFF_DK
    dv = DIFF_DV
    shift = int(math.log2(CHUNK))
    q0 = q_off + i * tq
    lo_vis = jnp.minimum(((q0 >> shift) + 1) * CHUNK, s_len)
    hi_vis = jnp.minimum((((q0 + tq - 1) >> shift) + 1) * CHUNK, s_len)
    n_full = lo_vis // tk
    n_tot = (hi_vis + tk - 1) // tk
    chains = [(r, c) for r in range(tq // tqs) for c in range(2)]
    lane_tiles = tk // LANES if tk % LANES == 0 else 0

    m_ref[...] = jnp.full(m_ref.shape, NEG_BIG, F32)
    l_ref[...] = jnp.zeros(l_ref.shape, F32)
    acc_ref[...] = jnp.zeros(acc_ref.shape, F32)

    def step(j, masked):
        k0 = pl.multiple_of(j * tk, tk)
        ks = k_ref[pl.ds(k0, tk), :]
        vs = v_ref[pl.ds(k0, tk), :]
        s, m_prev, m_new, p = {}, {}, {}, {}
        for ch in chains:
            r, c = ch
            s[ch] = lax.dot_general(q_ref[r * tqs:(r + 1) * tqs, c * dk:(c + 1) * dk],
                                    ks[:, c * dk:(c + 1) * dk],
                                    (((1,), (1,)), ((), ())), preferred_element_type=F32)
        if masked:
            for ch in chains:
                r, c = ch
                qpos = q0 + r * tqs + lax.broadcasted_iota(jnp.int32, s[ch].shape, 0)
                kpos = k0 + lax.broadcasted_iota(jnp.int32, s[ch].shape, 1)
                s[ch] = jnp.where((kpos >> shift) <= (qpos >> shift), s[ch], NEG_BIG)
        for n, ch in enumerate(chains):
            m_prev[ch] = m_ref[n]
            m_new[ch] = jnp.maximum(m_prev[ch], jnp.max(s[ch], axis=-1, keepdims=True))
        for n, ch in enumerate(chains):
            if lane_tiles:
                m_t = jnp.concatenate([m_new[ch]] * lane_tiles, axis=1)
            else:
                m_t = m_new[ch][:, :1]
            p[ch] = jnp.exp2(s[ch] - m_t)
        for n, ch in enumerate(chains):
            alpha = jnp.exp2(m_prev[ch] - m_new[ch])
            l_ref[n] = alpha * l_ref[n] + jnp.sum(p[ch], axis=-1, keepdims=True)
            pv = jnp.dot(p[ch].astype(BF16), vs, preferred_element_type=F32)
            acc_ref[n] = jnp.concatenate([alpha] * (dv // LANES), axis=1) * acc_ref[n] + pv
            m_ref[n] = m_new[ch]

    def body_full(j, carry):
        step(j, False)
        return carry

    def body_masked(j, carry):
        step(j, True)
        return carry

    lax.fori_loop(0, n_full, body_full, 0)
    lax.fori_loop(n_full, n_tot, body_masked, 0)

    lam = (jnp.exp(jnp.sum(lam_ref[0:1, :] * lam_ref[1:2, :], axis=-1, keepdims=True))
           - jnp.exp(jnp.sum(lam_ref[2:3, :] * lam_ref[3:4, :], axis=-1, keepdims=True)) + lam_init)
    for r in range(tq // tqs):
        l0 = jnp.concatenate([l_ref[2 * r]] * (dv // LANES), axis=1)
        l1 = jnp.concatenate([l_ref[2 * r + 1]] * (dv // LANES), axis=1)
        o = acc_ref[2 * r] / l0 - lam * (acc_ref[2 * r + 1] / l1)
        o = (o * lax.rsqrt(jnp.mean(o * o, axis=-1, keepdims=True) + EPS)) * sw_ref[...]
        o_ref[r * tqs:(r + 1) * tqs, :] = (o * (1.0 - lam_init)).astype(o_ref.dtype)


def _attn(q, k, v, lam4, subln_w, nb, t_len, s_len, tq, tqs, tk, q_off, lam_init):
    m = q.shape[0]
    nq = t_len // tq
    dv = DIFF_DV
    n_chains = 2 * (tq // tqs)
    return pl.pallas_call(
        functools.partial(_attn_kernel, tk=tk, tqs=tqs, q_off=q_off, lam_init=lam_init),
        grid=(nb, DIFF_H, nq),
        in_specs=[
            pl.BlockSpec((tq, dv), lambda b, h, i: (b * nq + i, h)),
            pl.BlockSpec((s_len, dv), lambda b, h, i: (b, h)),
            pl.BlockSpec((s_len, dv), lambda b, h, i: (b, h)),
            pl.BlockSpec((4, DIFF_DK), lambda b, h, i: (0, 0)),
            pl.BlockSpec((1, dv), lambda b, h, i: (0, 0)),
        ],
        out_specs=pl.BlockSpec((tq, dv), lambda b, h, i: (b * nq + i, h)),
        out_shape=jax.ShapeDtypeStruct((m, DIFF_H * dv), BF16),
        scratch_shapes=[pltpu.VMEM((n_chains, tqs, dv), F32), pltpu.VMEM((n_chains, tqs, LANES), F32),
                        pltpu.VMEM((n_chains, tqs, LANES), F32)],
        compiler_params=_cparams(("arbitrary", "arbitrary", "arbitrary")),
        name="diff_attn",
    )(q, k, v, lam4, subln_w)


def _attn_cached_kernel(q_ref, kn_ref, vn_ref, kc_ref, vc_ref, lam_ref, sw_ref, o_ref,
                        *, q_off, lam_init, n_heads):
    t = q_ref.shape[0]
    rows = kc_ref.shape[0]
    past = rows // n_heads
    dk = DIFF_DK
    dv = DIFF_DV
    shift = int(math.log2(CHUNK))
    head_shift = int(math.log2(n_heads))
    q_chunk = (q_off + lax.broadcasted_iota(jnp.int32, (t, 1), 0)) >> shift
    col = lax.broadcasted_iota(jnp.int32, (t, rows), 1)
    col_head = col & (n_heads - 1)
    vis_pos = ((col >> head_shift) >> shift) <= q_chunk
    vis_n = ((past + lax.broadcasted_iota(jnp.int32, (t, t), 1)) >> shift) <= q_chunk
    lam = (jnp.exp(jnp.sum(lam_ref[0:1, :] * lam_ref[1:2, :], axis=-1, keepdims=True))
           - jnp.exp(jnp.sum(lam_ref[2:3, :] * lam_ref[3:4, :], axis=-1, keepdims=True)) + lam_init)
    nt = (((1,), (1,)), ((), ()))
    kc = kc_ref[...].astype(BF16)
    vc = vc_ref[...].astype(BF16)
    for h in range(n_heads):
        vis_c = vis_pos & (col_head == h)
        hs = slice(h * dv, (h + 1) * dv)
        q = q_ref[:, hs]
        kn = kn_ref[:, hs]
        vn = vn_ref[:, hs]
        outs = []
        for c in range(2):
            cs = slice(c * dk, (c + 1) * dk)
            sc = jnp.where(vis_c, lax.dot_general(q[:, cs], kc[:, cs], nt, preferred_element_type=F32), NEG_BIG)
            sn = jnp.where(vis_n, lax.dot_general(q[:, cs], kn[:, cs], nt, preferred_element_type=F32), NEG_BIG)
            m = jnp.maximum(jnp.max(sc, axis=-1, keepdims=True), jnp.max(sn, axis=-1, keepdims=True))
            pc = jnp.exp2(sc - m)
            pn = jnp.exp2(sn - m)
            l = jnp.sum(pc, axis=-1, keepdims=True) + jnp.sum(pn, axis=-1, keepdims=True)
            acc = (jnp.dot(pc.astype(BF16), vc, preferred_element_type=F32)
                   + jnp.dot(pn.astype(BF16), vn, preferred_element_type=F32))
            outs.append(acc / l)
        o = outs[0] - lam * outs[1]
        o = (o * lax.rsqrt(jnp.mean(o * o, axis=-1, keepdims=True) + EPS)) * sw_ref[...]
        o_ref[:, hs] = (o * (1.0 - lam_init)).astype(o_ref.dtype)


def _attn_cached(q, k_new, v_new, k_cache, v_cache, lam4, subln_w, nb, t_len, q_off, lam_init):
    past, n_heads, dv = k_cache.shape[1:]
    assert n_heads & (n_heads - 1) == 0 and past >= 1
    width = n_heads * dv
    rows = past * n_heads
    seq = lambda b: (b, 0)
    return pl.pallas_call(
        functools.partial(_attn_cached_kernel, q_off=q_off, lam_init=lam_init, n_heads=n_heads),
        grid=(nb,),
        in_specs=[
            pl.BlockSpec((t_len, width), seq),
            pl.BlockSpec((t_len, width), seq),
            pl.BlockSpec((t_len, width), seq),
            pl.BlockSpec((rows, dv), seq),
            pl.BlockSpec((rows, dv), seq),
            pl.BlockSpec((4, DIFF_DK), lambda b: (0, 0)),
            pl.BlockSpec((1, dv), lambda b: (0, 0)),
        ],
        out_specs=pl.BlockSpec((t_len, width), seq),
        out_shape=jax.ShapeDtypeStruct((nb * t_len, width), BF16),
        compiler_params=_cparams(("arbitrary",)),
        name="diff_attn_cached",
    )(q, k_new, v_new, k_cache.reshape(nb * rows, dv), v_cache.reshape(nb * rows, dv), lam4, subln_w)


def _merge_kernel(ya_ref, yb_ref, ga_ref, gb_ref, wa_ref, wb_ref, o_ref):
    pa = jnp.dot(ya_ref[...], wa_ref[...], preferred_element_type=F32)
    pb = jnp.dot(yb_ref[...], wb_ref[...], preferred_element_type=F32)
    o_ref[...] = (_sigmoid(ga_ref[...]) * pa + _sigmoid(gb_ref[...]) * pb).astype(o_ref.dtype)


def _merge(ya, yb, proj, wa, wb, off_ga, off_gb, tm, tn):
    m = ya.shape[0]
    n = wa.shape[1]
    return pl.pallas_call(
        _merge_kernel,
        grid=(m // tm, n // tn),
        in_specs=[
            pl.BlockSpec((tm, ya.shape[1]), lambda i, j: (i, 0)),
            pl.BlockSpec((tm, yb.shape[1]), lambda i, j: (i, 0)),
            pl.BlockSpec((tm, tn), lambda i, j: (i, off_ga // tn + j)),
            pl.BlockSpec((tm, tn), lambda i, j: (i, off_gb // tn + j)),
            pl.BlockSpec((wa.shape[0], tn), lambda i, j: (0, j)),
            pl.BlockSpec((wb.shape[0], tn), lambda i, j: (0, j)),
        ],
        out_specs=pl.BlockSpec((tm, tn), lambda i, j: (i, j)),
        out_shape=jax.ShapeDtypeStruct((m, n), BF16),
        compiler_params=_cparams(("arbitrary", "arbitrary")),
        name="merge",
    )(ya, yb, proj, proj, wa, wb)


def _outproj_kernel(mg_ref, x_ref, w_ref, o_ref):
    o_ref[...] = x_ref[...] + jnp.dot(mg_ref[...], w_ref[...], preferred_element_type=F32)


def _outproj(merged, x, w, tm, tn):
    m, d = x.shape
    return pl.pallas_call(
        _outproj_kernel,
        grid=(m // tm, d // tn),
        in_specs=[
            pl.BlockSpec((tm, merged.shape[1]), lambda i, j: (i, 0)),
            pl.BlockSpec((tm, tn), lambda i, j: (i, j)),
            pl.BlockSpec((w.shape[0], tn), lambda i, j: (0, j)),
        ],
        out_specs=pl.BlockSpec((tm, tn), lambda i, j: (i, j)),
        out_shape=jax.ShapeDtypeStruct((m, d), F32),
        compiler_params=_cparams(("arbitrary", "arbitrary")),
        name="out_proj",
    )(merged, x, w)


def _ffn_kernel(h_ref, nw_ref, wg_ref, wu_ref, cw_ref, cb_ref, wd_ref, hist_ref, nf_ref,
                y_ref, tail_ref, hn_scr, acc_scr, carry_scr, *, n_seq, tiles_per_seq,
                final_norm):
    i = pl.program_id(0)
    j = pl.program_id(1)
    nj = pl.num_programs(1)
    tm = h_ref.shape[0]
    tf = wg_ref.shape[1]
    ts = tm // n_seq

    @pl.when(j == 0)
    def _():
        hh = h_ref[...]
        ms = jnp.mean(hh * hh, axis=-1, keepdims=True)
        hn_scr[...] = ((hh * lax.rsqrt(ms + EPS)) * nw_ref[...]).astype(BF16)
        acc_scr[...] = jnp.zeros(acc_scr.shape, F32)

    hn = hn_scr[...]
    g = jnp.dot(hn, wg_ref[...], preferred_element_type=F32)
    up = jnp.dot(hn, wu_ref[...], preferred_element_type=F32)

    @pl.when((i % tiles_per_seq) == 0)
    def _():
        carry_scr[j] = hist_ref[...]

    w0 = cw_ref[0:1, :]
    w1 = cw_ref[1:2, :]
    w2 = cw_ref[2:3, :]
    cb = cb_ref[...]
    outs = []
    for s in range(n_seq):
        gs = g[s * ts:(s + 1) * ts]
        prev = carry_scr[j, s]
        conv = gs * w2 + _shift_rows(gs, prev, 1) * w1 + _shift_rows(gs, prev, 2) * w0 + cb
        outs.append(_silu(conv) * up[s * ts:(s + 1) * ts])
        tail = gs[ts - SUBLANES:]
        carry_scr[j, s] = tail
        tail_ref[s] = tail
    act = outs[0] if n_seq == 1 else jnp.concatenate(outs, axis=0)
    acc_scr[...] += jnp.dot(act.astype(BF16), wd_ref[...], preferred_element_type=F32)

    @pl.when(j == nj - 1)
    def _():
        o = h_ref[...] + acc_scr[...]
        if final_norm:
            ms = jnp.mean(o * o, axis=-1, keepdims=True)
            o = (o * lax.rsqrt(ms + EPS)) * nf_ref[...]
        y_ref[...] = o


def _ffn(h, norm_w, wg, wu, conv_w, conv_b, wd, hist8, norm_final, final_norm, t_len, tm, tf):
    m, d = h.shape
    f = wg.shape[1]
    nj = f // tf
    if tm >= t_len:
        n_seq, tiles_per_seq = tm // t_len, 1
    else:
        n_seq, tiles_per_seq = 1, t_len // tm
    n_tiles = m // tm
    seq_blk = lambda i, j: ((i // tiles_per_seq), 0, j)
    y, tails = pl.pallas_call(
        functools.partial(_ffn_kernel, n_seq=n_seq, tiles_per_seq=tiles_per_seq, final_norm=final_norm),
        grid=(n_tiles, nj),
        in_specs=[
            pl.BlockSpec((tm, d), lambda i, j: (i, 0)),
            pl.BlockSpec((1, d), lambda i, j: (0, 0)),
            pl.BlockSpec((d, tf), lambda i, j: (0, j)),
            pl.BlockSpec((d, tf), lambda i, j: (0, j)),
            pl.BlockSpec((FFN_CONV, tf), lambda i, j: (0, j)),
            pl.BlockSpec((1, tf), lambda i, j: (0, j)),
            pl.BlockSpec((tf, d), lambda i, j: (j, 0)),
            pl.BlockSpec((n_seq, SUBLANES, tf), seq_blk),
            pl.BlockSpec((1, d), lambda i, j: (0, 0)),
        ],
        out_specs=[
            pl.BlockSpec((tm, d), lambda i, j: (i, 0)),
            pl.BlockSpec((n_seq, SUBLANES, tf), lambda i, j: (i, 0, j)),
        ],
        out_shape=[jax.ShapeDtypeStruct((m, d), F32),
                   jax.ShapeDtypeStruct((n_tiles * n_seq, SUBLANES, f), F32)],
        scratch_shapes=[pltpu.VMEM((tm, d), BF16), pltpu.VMEM((tm, d), F32),
                        pltpu.VMEM((nj, n_seq, SUBLANES, tf), F32)],
        compiler_params=_cparams(("arbitrary", "arbitrary")),
        name="conv_ffn",
    )(h, norm_w, wg, wu, conv_w, conv_b, wd, hist8, norm_final)
    return y, tails


def _pad_hist(hist):
    b, r, c = hist.shape
    return jnp.concatenate([jnp.zeros((b, SUBLANES - r, c), hist.dtype), hist], axis=1)


def _pick(n, cands):
    for c in cands:
        if n % c == 0:
            return c
    return n


def _layer(x, pos_offset, k_hist, v_hist, gdn_conv_hist, gdn_s0, ffn_hist, lam_init, wts, norm_final,
           final_norm):
    nb, t_len, d_model = x.shape
    m = nb * t_len
    past = k_hist.shape[1]
    x2 = x.reshape(m, d_model)
    qk_dim = GDN_HK * GDN_D
    v_dim = GDN_HV * GDN_D
    dqk = DIFF_H * 2 * DIFF_DK
    off = {"gq": 0, "gk": qk_dim, "gv": 2 * qk_dim, "gz": 2 * qk_dim + v_dim}
    off["dq"] = off["gz"] + v_dim
    off["dk"] = off["dq"] + dqk
    off["dv"] = off["dk"] + dqk
    off["ga"] = off["dv"] + DIFF_H * DIFF_DV
    off["gb"] = off["ga"] + d_model

    tm = _pick(m, (1024, 512, 256))
    tn = _pick(wts["w_main"].shape[1], (1024, 512, 256, 128))
    proj, gates = _inproj(x2, wts["norm_mix"], wts["w_main"], wts["w_gates"], tm, tn)

    tmr = _pick(m, (256,))
    pos = pos_offset + jnp.arange(t_len, dtype=jnp.int32)
    tables = _rope_tables(pos)
    if t_len < tmr:
        tables = [jnp.tile(tb, (tmr // t_len, 1)) for tb in tables]
    n_tab = max(t_len // tmr, 1)
    q_r, k_rows, k_b, v_rows, v_b = _rope(proj, tables, tmr, n_tab, off["dq"], off["dk"], off["dv"], dqk)

    chunk = CHUNK if t_len % CHUNK == 0 else t_len
    tt = _pick(t_len, (256,)) if t_len % CHUNK == 0 else t_len
    kg = _pick(GDN_HK, (4, 2))
    ya, gdn_s = _gdn(proj, gates, _pad_hist(gdn_conv_hist), wts["gdn_conv_w"], wts["alog"], wts["dtb"],
                     wts["gdn_norm_w"], gdn_s0, nb, t_len, tt, chunk, kg)
    qkv_tail = proj.reshape(nb, t_len, -1)[:, t_len - (GDN_CONV - 1):, :off["gz"]]
    if t_len >= GDN_CONV - 1:
        gdn_conv_new = qkv_tail
    else:
        gdn_conv_new = jnp.concatenate([gdn_conv_hist, proj.reshape(nb, t_len, -1)[:, :, :off["gz"]]],
                                       axis=1)[:, -(GDN_CONV - 1):]

    if past:
        yb = _attn_cached(q_r, k_b, v_b, k_hist, v_hist, wts["lam4"], wts["subln_w"], nb, t_len,
                          pos_offset, lam_init)
    else:
        tq = _pick(t_len, (512, 256))
        tqs = _pick(tq, (256,))
        tk = _pick(t_len, (512, 256))
        yb = _attn(q_r, k_b, v_b, wts["lam4"], wts["subln_w"], nb, t_len, t_len, tq, tqs, tk,
                   pos_offset, lam_init)

    tmm = _pick(m, (1024, 512, 256))
    tnm = _pick(d_model, (512, 256, 128))
    merged = _merge(ya, yb, proj, wts["w_br_a"], wts["w_br_b"], off["ga"], off["gb"], tmm, tnm)
    h = _outproj(merged, x2, wts["w_out"], tmm, tnm)
    tmf = _pick(m, (512, 256))
    y, tails = _ffn(h, wts["norm_ffn"], wts["w_gate"], wts["w_up"], wts["ffn_conv_w"], wts["ffn_conv_b"],
                    wts["w_down"], _pad_hist(ffn_hist), norm_final, final_norm, t_len, tmf, 512)
    d_ff = tails.shape[-1]
    if tmf >= t_len:
        seq_tails = tails
    else:
        seq_tails = tails.reshape(nb, t_len // tmf, SUBLANES, d_ff)[:, -1]
    ffn_new = seq_tails[:, SUBLANES - (FFN_CONV - 1):]

    return (y.reshape(nb, t_len, d_model),
            k_rows.reshape(nb, t_len, DIFF_H, 2 * DIFF_DK),
            v_rows.reshape(nb, t_len, DIFF_H, DIFF_DV),
            gdn_conv_new, gdn_s, ffn_new)


def _prep_weights(l, norm_mix, w_in, gdn_conv_w, gdn_a_log, gdn_dt_bias, gdn_norm_w,
                  lq1, lk1, lq2, lk2, subln_w, w_branch, w_out, norm_ffn,
                  w_gate, w_up, ffn_conv_w, ffn_conv_b, w_down):
    qk_dim = GDN_HK * GDN_D
    v_dim = GDN_HV * GDN_D
    g0 = 2 * qk_dim + 2 * v_dim
    g1 = g0 + 2 * GDN_HV
    wi = w_in[l]
    d_model = wi.shape[0]
    w_main = jnp.concatenate([wi[:, :g0], wi[:, g1:]], axis=1).astype(BF16)
    w_gates = jnp.concatenate([wi[:, g0:g1], jnp.zeros((d_model, LANES - 2 * GDN_HV), wi.dtype)],
                              axis=1).astype(BF16)
    pad = jnp.zeros((LANES - GDN_HV,), F32)
    return {
        "norm_mix": norm_mix[l].reshape(1, -1),
        "w_main": w_main,
        "w_gates": w_gates,
        "gdn_conv_w": gdn_conv_w[l],
        "alog": jnp.concatenate([gdn_a_log[l].astype(F32), pad]).reshape(1, LANES),
        "dtb": jnp.concatenate([gdn_dt_bias[l].astype(F32), pad]).reshape(1, LANES),
        "gdn_norm_w": gdn_norm_w[l].reshape(1, -1),
        "lam4": jnp.stack([lq1[l], lk1[l], lq2[l], lk2[l]], axis=0).astype(F32),
        "subln_w": subln_w[l].reshape(1, -1),
        "w_br_a": w_branch[l][:v_dim].astype(BF16),
        "w_br_b": w_branch[l][v_dim:].astype(BF16),
        "w_out": w_out[l].astype(BF16),
        "norm_ffn": norm_ffn[l].reshape(1, -1),
        "w_gate": w_gate[l].astype(BF16),
        "w_up": w_up[l].astype(BF16),
        "ffn_conv_w": ffn_conv_w[l],
        "ffn_conv_b": ffn_conv_b[l].reshape(1, -1),
        "w_down": w_down[l].astype(BF16),
    }


def kernel(x_prompt, x_sample, cache_diff_k, cache_diff_v, state_gdn_conv, state_gdn_rec, state_ffn_conv,
           norm_mix, w_in, gdn_conv_w, gdn_a_log, gdn_dt_bias, gdn_norm_w,
           diff_lambda_q1, diff_lambda_k1, diff_lambda_q2, diff_lambda_k2, diff_subln_w,
           w_branch, w_out, norm_ffn, ffn_w_gate, ffn_w_up, ffn_conv_w, ffn_conv_b, ffn_w_down,
           norm_final):
    depth = w_in.shape[0]
    bp = x_prompt.shape[0]
    past = cache_diff_k.shape[2]
    dt = x_prompt.dtype
    d_ff = ffn_w_gate.shape[-1]
    conv_dim = gdn_conv_w.shape[-1]
    nf = norm_final.reshape(1, -1)
    hp, hs = x_prompt, x_sample
    p_out = [[], [], [], [], []]
    s_out = [[], [], [], [], []]
    for l in range(depth):
        lam_init = 0.8 - 0.6 * math.exp(-0.3 * l)
        wts = _prep_weights(l, norm_mix, w_in, gdn_conv_w, gdn_a_log, gdn_dt_bias, gdn_norm_w,
                            diff_lambda_q1, diff_lambda_k1, diff_lambda_q2, diff_lambda_k2, diff_subln_w,
                            w_branch, w_out, norm_ffn, ffn_w_gate, ffn_w_up, ffn_conv_w, ffn_conv_b,
                            ffn_w_down)
        last = l == depth - 1
        hp, *sp = _layer(hp, 0,
                         jnp.zeros((bp, 0, DIFF_H, 2 * DIFF_DK), dt),
                         jnp.zeros((bp, 0, DIFF_H, DIFF_DV), dt),
                         jnp.zeros((bp, GDN_CONV - 1, conv_dim), dt),
                         jnp.zeros((bp, GDN_HV, GDN_D, GDN_D), dt),
                         jnp.zeros((bp, FFN_CONV - 1, d_ff), dt),
                         lam_init, wts, nf, last)
        hs, *ss = _layer(hs, past, cache_diff_k[l], cache_diff_v[l], state_gdn_conv[l], state_gdn_rec[l],
                         state_ffn_conv[l], lam_init, wts, nf, last)
        for i in range(5):
            p_out[i].append(sp[i])
            s_out[i].append(ss[i])
    p_k, p_v, p_gdn_conv, p_gdn_rec, p_ffn_conv = [jnp.stack(a, axis=0) for a in p_out]
    s_k, s_v, s_gdn_conv, s_gdn_rec, s_ffn_conv = [jnp.stack(a, axis=0) for a in s_out]
    return (hp, hs, p_k, p_v, p_gdn_conv, p_gdn_rec, p_ffn_conv,
            s_k, s_v, s_gdn_conv, s_gdn_rec, s_ffn_conv)
```

```python
import functools
import math

import jax
import jax.numpy as jnp
from jax import lax
from jax.experimental import pallas as pl
from jax.experimental.pallas import tpu as pltpu

F32 = jnp.float32
BF16 = jnp.bfloat16

CHUNK = 64
EPS = 1e-6
GDN_HK = 16
GDN_HV = 32
GDN_D = 128
GDN_CONV = 4
DIFF_H = 8
DIFF_DK = 128
DIFF_DV = 2 * DIFF_DK
ROPE_THETA = 500000.0
ROT_DIM = DIFF_DK // 4
FFN_CONV = 3
LANES = 128
SUBLANES = 8
NEG_BIG = -1e30
VMEM_LIMIT = 56 * 1024 * 1024


def _cparams(sem):
    return pltpu.CompilerParams(dimension_semantics=sem, vmem_limit_bytes=VMEM_LIMIT)


def _sigmoid(x):
    return 1.0 / (1.0 + jnp.exp(-x))


def _silu(x):
    return x * _sigmoid(x)


def _bdot(a, b):
    return jnp.dot(a.astype(BF16), b.astype(BF16), preferred_element_type=F32)


def _bdot_nt(a, b):
    return lax.dot_general(a.astype(BF16), b.astype(BF16), (((1,), (1,)), ((), ())),
                           preferred_element_type=F32)


def _shift_rows(x, prev8, s):
    n = x.shape[0]
    xr = pltpu.roll(x, s, axis=0)
    pr = pltpu.roll(prev8, s, axis=0)
    row = lax.broadcasted_iota(jnp.int32, pr.shape, 0)
    head = jnp.where(row < s, pr, xr[:SUBLANES])
    if n == SUBLANES:
        return head
    return jnp.concatenate([head, xr[SUBLANES:]], axis=0)


def _inproj_kernel(x_ref, nw_ref, w_ref, wg_ref, o_ref, og_ref, xn_ref):
    @pl.when(pl.program_id(1) == 0)
    def _():
        x = x_ref[...]
        ms = jnp.mean(x * x, axis=-1, keepdims=True)
        xn = ((x * lax.rsqrt(ms + EPS)) * nw_ref[...]).astype(BF16)
        xn_ref[...] = xn
        og_ref[...] = jnp.dot(xn, wg_ref[...], preferred_element_type=F32)

    o_ref[...] = jnp.dot(xn_ref[...], w_ref[...], preferred_element_type=F32)


def _inproj(x, norm_w, w_main, w_gates, tm, tn):
    m, d = x.shape
    n = w_main.shape[1]
    return pl.pallas_call(
        _inproj_kernel,
        grid=(m // tm, n // tn),
        in_specs=[
            pl.BlockSpec((tm, d), lambda i, j: (i, 0)),
            pl.BlockSpec((1, d), lambda i, j: (0, 0)),
            pl.BlockSpec((d, tn), lambda i, j: (0, j)),
            pl.BlockSpec((d, LANES), lambda i, j: (0, 0)),
        ],
        out_specs=[
            pl.BlockSpec((tm, tn), lambda i, j: (i, j)),
            pl.BlockSpec((tm, LANES), lambda i, j: (i, 0)),
        ],
        out_shape=[jax.ShapeDtypeStruct((m, n), F32), jax.ShapeDtypeStruct((m, LANES), F32)],
        scratch_shapes=[pltpu.VMEM((tm, d), BF16)],
        compiler_params=_cparams(("arbitrary", "arbitrary")),
        name="in_proj",
    )(x, norm_w, w_main, w_gates)


def _rope_kernel(dq_ref, dk_ref, dv_ref, c_ref, s1_ref, s2_ref,
                 q_ref, kf_ref, kb_ref, vf_ref, vb_ref, *, scale):
    c = c_ref[...]
    s1 = s1_ref[...]
    s2 = s2_ref[...]
    half = ROT_DIM // 2
    for g in range(dq_ref.shape[1] // LANES):
        sl = slice(g * LANES, (g + 1) * LANES)
        xq = dq_ref[:, sl]
        rq = xq * c + pltpu.roll(xq, LANES - half, axis=1) * s1 + pltpu.roll(xq, half, axis=1) * s2
        q_ref[:, sl] = (rq * scale).astype(BF16)
        xk = dk_ref[:, sl]
        rk = xk * c + pltpu.roll(xk, LANES - half, axis=1) * s1 + pltpu.roll(xk, half, axis=1) * s2
        kf_ref[:, sl] = rk
        kb_ref[:, sl] = rk.astype(BF16)
    xv = dv_ref[...]
    vf_ref[...] = xv
    vb_ref[...] = xv.astype(BF16)


def _rope_tables(pos):
    half = ROT_DIM // 2
    inv = jnp.exp(-math.log(ROPE_THETA) * jnp.arange(0, ROT_DIM, 2, dtype=F32) / ROT_DIM)
    ang = pos.astype(F32)[:, None] * inv[None, :]
    cos, sin = jnp.cos(ang), jnp.sin(ang)
    n = pos.shape[0]
    ones = jnp.ones((n, LANES - ROT_DIM), F32)
    zeros = jnp.zeros((n, LANES - half), F32)
    c = jnp.concatenate([cos, cos, ones], axis=1)
    s1 = jnp.concatenate([-sin, zeros], axis=1)
    s2 = jnp.concatenate([jnp.zeros((n, half), F32), sin, jnp.zeros((n, LANES - ROT_DIM), F32)], axis=1)
    return c, s1, s2


def _rope(proj, tables, tm, n_tab_blocks, off_q, off_k, off_v, width):
    m = proj.shape[0]
    wb = width
    row = lambda i: (i, 0)
    tab = lambda i: (i % n_tab_blocks, 0)
    outs = pl.pallas_call(
        functools.partial(_rope_kernel, scale=DIFF_DK ** -0.5 * math.log2(math.e)),
        grid=(m // tm,),
        in_specs=[
            pl.BlockSpec((tm, wb), lambda i: (i, off_q // wb)),
            pl.BlockSpec((tm, wb), lambda i: (i, off_k // wb)),
            pl.BlockSpec((tm, wb), lambda i: (i, off_v // wb)),
            pl.BlockSpec((tm, LANES), tab),
            pl.BlockSpec((tm, LANES), tab),
            pl.BlockSpec((tm, LANES), tab),
        ],
        out_specs=[pl.BlockSpec((tm, wb), row)] * 5,
        out_shape=[jax.ShapeDtypeStruct((m, wb), BF16), jax.ShapeDtypeStruct((m, wb), F32),
                   jax.ShapeDtypeStruct((m, wb), BF16), jax.ShapeDtypeStruct((m, wb), F32),
                   jax.ShapeDtypeStruct((m, wb), BF16)],
        compiler_params=_cparams(("arbitrary",)),
        name="rope",
    )(proj, proj, proj, *tables)
    return outs


def _split2(x):
    hi = x.astype(BF16)
    lo = (x - hi.astype(F32)).astype(BF16)
    return hi, lo


def _gdn_kernel(q_ref, k_ref, v_ref, z_ref, g_ref, hq_ref, hk_ref, hv_ref,
                cwq_ref, cwk_ref, cwv_ref, alog_ref, dtb_ref, nw_ref, s0_ref,
                o_ref, sfin_ref, s_scr, pq_scr, pk_scr, pv_scr, *, chunk, kg):
    h = pl.program_id(1)
    t = pl.program_id(2)
    nt = pl.num_programs(2)
    tt = q_ref.shape[0]
    d = GDN_D
    c = chunk

    @pl.when(t == 0)
    def _():
        s_scr[...] = s0_ref[0]
        pq_scr[...] = hq_ref[0]
        pk_scr[...] = hk_ref[0]
        pv_scr[...] = hv_ref[0]

    def conv_silu(x, prev, w_ref):
        acc = x * w_ref[GDN_CONV - 1:GDN_CONV, :]
        for s in range(1, GDN_CONV):
            acc = acc + _shift_rows(x, prev, s) * w_ref[GDN_CONV - 1 - s:GDN_CONV - s, :]
        return _silu(acc)

    xq = q_ref[...]
    xk = k_ref[...]
    xv = v_ref[...]
    qc = conv_silu(xq, pq_scr[...], cwq_ref)
    kc = conv_silu(xk, pk_scr[...], cwk_ref)
    v = conv_silu(xv, pv_scr[...], cwv_ref)
    pq_scr[...] = xq[tt - SUBLANES:]
    pk_scr[...] = xk[tt - SUBLANES:]
    pv_scr[...] = xv[tt - SUBLANES:]

    q, k = [], []
    for kh in range(kg):
        qh = qc[:, kh * d:(kh + 1) * d]
        kh_ = kc[:, kh * d:(kh + 1) * d]
        q.append(qh * lax.rsqrt(jnp.sum(qh * qh, axis=-1, keepdims=True) + EPS) * (d ** -0.5))
        k.append(kh_ * lax.rsqrt(jnp.sum(kh_ * kh_, axis=-1, keepdims=True) + EPS))

    gates = g_ref[...]
    xg = gates + dtb_ref[...]
    softplus = jnp.maximum(xg, 0.0) + jnp.log(1.0 + jnp.exp(-jnp.abs(xg)))
    g_all = -jnp.exp(alog_ref[...]) * softplus
    beta_all = _sigmoid(gates)
    lane = lax.broadcasted_iota(jnp.int32, gates.shape, 1)

    ri = lax.broadcasted_iota(jnp.int32, (c, c), 0)
    ci = lax.broadcasted_iota(jnp.int32, (c, c), 1)
    causal = ri >= ci
    strict = ri > ci
    tril_b = causal.astype(BF16)
    eye = (ri == ci).astype(F32)
    rm = lax.broadcasted_iota(jnp.int32, (c, 2 * LANES), 0)
    cm = lax.broadcasted_iota(jnp.int32, (c, 2 * LANES), 1)
    cum_mask = jnp.where(((cm < c) & (rm > cm)) | (cm >= LANES), 1.0, 0.0)

    z = z_ref[...]
    nw = nw_ref[...]
    n_sq = int(math.log2(c)) - 1
    n_chunks = tt // c
    heads = range(2 * kg)
    items = [(e, n) for e in heads for n in range(n_chunks)]
    rows = [slice(n * c, (n + 1) * c) for n in range(n_chunks)]

    g_cols, b_cols = [], []
    for e in heads:
        hv = 2 * kg * h + e
        g_cols.append(jnp.sum(jnp.where(lane == hv, g_all, 0.0), axis=-1, keepdims=True))
        b_cols.append(jnp.sum(jnp.where(lane == GDN_HV + hv, beta_all, 0.0), axis=-1, keepdims=True))

    gd = {}
    for e, n in items:
        gc = jnp.broadcast_to(g_cols[e][rows[n]], (c, 2 * LANES)) * cum_mask
        g_hi, g_lo = _split2(gc)
        gd[e, n] = (jnp.dot(tril_b, g_hi, preferred_element_type=F32)
                    + jnp.dot(tril_b, g_lo, preferred_element_type=F32))
    kk = {(kh, n): _bdot_nt(k[kh][r], k[kh][r]) for kh in range(kg) for n, r in enumerate(rows)}
    qk = {(kh, n): _bdot_nt(q[kh][r], k[kh][r]) for kh in range(kg) for n, r in enumerate(rows)}

    x, tinv, rhs, intra, qg, kdt, exp_gl = {}, {}, {}, {}, {}, {}, {}
    for e, n in items:
        r = rows[n]
        kh = e // 2
        gb = gd[e, n][:, LANES:]
        decay = jnp.where(causal, jnp.exp(gd[e, n][:, :c]), 0.0)
        exp_g = jnp.exp(gb)
        g_last = gb[c - 1:c, :]
        exp_gl[e, n] = jnp.exp(g_last)
        bb = jnp.broadcast_to(b_cols[e][r], (c, LANES))
        a = jnp.where(strict, -(kk[kh, n] * bb[:, :c] * decay), 0.0)
        x[e, n] = a
        tinv[e, n] = eye + a
        kb = k[kh][r] * bb
        rhs[e, n] = jnp.concatenate([v[r, e * d:(e + 1) * d] * bb, kb * exp_g], axis=1)
        intra[e, n] = jnp.where(causal, qk[kh, n] * decay, 0.0)
        qg[e, n] = q[kh][r] * exp_g
        kdt[e, n] = (k[kh][r] * jnp.exp(g_last - gb)).T

    for it in items:
        x[it] = _bdot(x[it], x[it])
    for _ in range(n_sq - 1):
        for it in items:
            both = _bdot(jnp.concatenate([x[it], tinv[it]], axis=0), x[it])
            x[it] = both[:c]
            tinv[it] = tinv[it] + both[c:]
    for it in items:
        tinv[it] = tinv[it] + _bdot(tinv[it], x[it])

    u, w, kw, ku = {}, {}, {}, {}
    for it in items:
        uw = _bdot(tinv[it], rhs[it])
        u[it], w[it] = uw[:, :d], uw[:, d:]
    for it in items:
        kwu = _bdot(kdt[it], jnp.concatenate([w[it], u[it]], axis=1))
        kw[it], ku[it] = kwu[:, :d], kwu[:, d:]

    s_in = {}
    s_cur = [s_scr[e] for e in heads]
    for n in range(n_chunks):
        for e in heads:
            s_in[e, n] = s_cur[e]
            s_cur[e] = s_cur[e] * exp_gl[e, n] + ku[e, n] - _bdot(kw[e, n], s_cur[e])
    for e in heads:
        s_scr[e] = s_cur[e]

    for it in items:
        e, n = it
        ws = _bdot(jnp.concatenate([w[it], qg[it]], axis=0), s_in[it])
        v_new = u[it] - ws[:c]
        o = ws[c:] + _bdot(intra[it], v_new)
        o = o * lax.rsqrt(jnp.mean(o * o, axis=-1, keepdims=True) + EPS) * nw
        o = o * _silu(z[rows[n], e * d:(e + 1) * d])
        o_ref[rows[n], e * d:(e + 1) * d] = o.astype(o_ref.dtype)

    @pl.when(t == nt - 1)
    def _():
        sfin_ref[0] = s_scr[...]


def _gdn(proj, gates, hist8, conv_w, alog, dtb, nw, s0, nb, t_len, tt, chunk, kg):
    m = proj.shape[0]
    nt = t_len // tt
    d = GDN_D
    wqk = kg * d
    wv = 2 * kg * d
    nhb = GDN_HK // kg
    row = lambda b, h, t: b * nt + t
    ya, sfin = pl.pallas_call(
        functools.partial(_gdn_kernel, chunk=chunk, kg=kg),
        grid=(nb, nhb, nt),
        in_specs=[
            pl.BlockSpec((tt, wqk), lambda b, h, t: (row(b, h, t), h)),
            pl.BlockSpec((tt, wqk), lambda b, h, t: (row(b, h, t), nhb + h)),
            pl.BlockSpec((tt, wv), lambda b, h, t: (row(b, h, t), nhb + h)),
            pl.BlockSpec((tt, wv), lambda b, h, t: (row(b, h, t), 2 * nhb + h)),
            pl.BlockSpec((tt, LANES), lambda b, h, t: (row(b, h, t), 0)),
            pl.BlockSpec((1, SUBLANES, wqk), lambda b, h, t: (b, 0, h)),
            pl.BlockSpec((1, SUBLANES, wqk), lambda b, h, t: (b, 0, nhb + h)),
            pl.BlockSpec((1, SUBLANES, wv), lambda b, h, t: (b, 0, nhb + h)),
            pl.BlockSpec((GDN_CONV, wqk), lambda b, h, t: (0, h)),
            pl.BlockSpec((GDN_CONV, wqk), lambda b, h, t: (0, nhb + h)),
            pl.BlockSpec((GDN_CONV, wv), lambda b, h, t: (0, nhb + h)),
            pl.BlockSpec((1, LANES), lambda b, h, t: (0, 0)),
            pl.BlockSpec((1, LANES), lambda b, h, t: (0, 0)),
            pl.BlockSpec((1, d), lambda b, h, t: (0, 0)),
            pl.BlockSpec((1, 2 * kg, d, d), lambda b, h, t: (b, h, 0, 0)),
        ],
        out_specs=[
            pl.BlockSpec((tt, wv), lambda b, h, t: (row(b, h, t), h)),
            pl.BlockSpec((1, 2 * kg, d, d), lambda b, h, t: (b, h, 0, 0)),
        ],
        out_shape=[jax.ShapeDtypeStruct((m, GDN_HV * d), BF16),
                   jax.ShapeDtypeStruct((nb, GDN_HV, d, d), F32)],
        scratch_shapes=[pltpu.VMEM((2 * kg, d, d), F32), pltpu.VMEM((SUBLANES, wqk), F32),
                        pltpu.VMEM((SUBLANES, wqk), F32), pltpu.VMEM((SUBLANES, wv), F32)],
        compiler_params=_cparams(("arbitrary", "arbitrary", "arbitrary")),
        name="gdn",
    )(proj, proj, proj, proj, gates, hist8, hist8, hist8, conv_w, conv_w, conv_w,
      alog, dtb, nw, s0)
    return ya, sfin


def _attn_kernel(q_ref, k_ref, v_ref, lam_ref, sw_ref, o_ref, acc_ref, m_ref, l_ref,
                 *, tk, tqs, q_off, lam_init):
    i = pl.program_id(2)
    tq = q_ref.shape[0]
    s_len = k_ref.shape[0]
    dk = DIFF_DK
    dv = DIFF_DV
    shift = int(math.log2(CHUNK))
    q0 = q_off + i * tq
    lo_vis = jnp.minimum(((q0 >> shift) + 1) * CHUNK, s_len)
    hi_vis = jnp.minimum((((q0 + tq - 1) >> shift) + 1) * CHUNK, s_len)
    n_full = lo_vis // tk
    n_tot = (hi_vis + tk - 1) // tk
    chains = [(r, c) for r in range(tq // tqs) for c in range(2)]
    lane_tiles = tk // LANES if tk % LANES == 0 else 0

    m_ref[...] = jnp.full(m_ref.shape, NEG_BIG, F32)
    l_ref[...] = jnp.zeros(l_ref.shape, F32)
    acc_ref[...] = jnp.zeros(acc_ref.shape, F32)

    def step(j, masked):
        k0 = pl.multiple_of(j * tk, tk)
        ks = k_ref[pl.ds(k0, tk), :]
        vs = v_ref[pl.ds(k0, tk), :]
        s, m_prev, m_new, p = {}, {}, {}, {}
        for ch in chains:
            r, c = ch
            s[ch] = lax.dot_general(q_ref[r * tqs:(r + 1) * tqs, c * dk:(c + 1) * dk],
                                    ks[:, c * dk:(c + 1) * dk],
                                    (((1,), (1,)), ((), ())), preferred_element_type=F32)
        if masked:
            for ch in chains:
                r, c = ch
                qpos = q0 + r * tqs + lax.broadcasted_iota(jnp.int32, s[ch].shape, 0)
                kpos = k0 + lax.broadcasted_iota(jnp.int32, s[ch].shape, 1)
                s[ch] = jnp.where((kpos >> shift) <= (qpos >> shift), s[ch], NEG_BIG)
        for n, ch in enumerate(chains):
            m_prev[ch] = m_ref[n]
            m_new[ch] = jnp.maximum(m_prev[ch], jnp.max(s[ch], axis=-1, keepdims=True))
        for n, ch in enumerate(chains):
            if lane_tiles:
                m_t = jnp.concatenate([m_new[ch]] * lane_tiles, axis=1)
            else:
                m_t = m_new[ch][:, :1]
            p[ch] = jnp.exp2(s[ch] - m_t)
        for n, ch in enumerate(chains):
            alpha = jnp.exp2(m_prev[ch] - m_new[ch])
            l_ref[n] = alpha * l_ref[n] + jnp.sum(p[ch], axis=-1, keepdims=True)
            pv = jnp.dot(p[ch].astype(BF16), vs, preferred_element_type=F32)
            acc_ref[n] = jnp.concatenate([alpha] * (dv // LANES), axis=1) * acc_ref[n] + pv
            m_ref[n] = m_new[ch]

    def body_full(j, carry):
        step(j, False)
        return carry

    def body_masked(j, carry):
        step(j, True)
        return carry

    lax.fori_loop(0, n_full, body_full, 0)
    lax.fori_loop(n_full, n_tot, body_masked, 0)

    lam = (jnp.exp(jnp.sum(lam_ref[0:1, :] * lam_ref[1:2, :], axis=-1, keepdims=True))
           - jnp.exp(jnp.sum(lam_ref[2:3, :] * lam_ref[3:4, :], axis=-1, keepdims=True)) + lam_init)
    for r in range(tq // tqs):
        l0 = jnp.concatenate([l_ref[2 * r]] * (dv // LANES), axis=1)
        l1 = jnp.concatenate([l_ref[2 * r + 1]] * (dv // LANES), axis=1)
        o = acc_ref[2 * r] / l0 - lam * (acc_ref[2 * r + 1] / l1)
        o = (o * lax.rsqrt(jnp.mean(o * o, axis=-1, keepdims=True) + EPS)) * sw_ref[...]
        o_ref[r * tqs:(r + 1) * tqs, :] = (o * (1.0 - lam_init)).astype(o_ref.dtype)


def _attn(q, k, v, lam4, subln_w, nb, t_len, s_len, tq, tqs, tk, q_off, lam_init):
    m = q.shape[0]
    nq = t_len // tq
    dv = DIFF_DV
    n_chains = 2 * (tq // tqs)
    return pl.pallas_call(
        functools.partial(_attn_kernel, tk=tk, tqs=tqs, q_off=q_off, lam_init=lam_init),
        grid=(nb, DIFF_H, nq),
        in_specs=[
            pl.BlockSpec((tq, dv), lambda b, h, i: (b * nq + i, h)),
            pl.BlockSpec((s_len, dv), lambda b, h, i: (b, h)),
            pl.BlockSpec((s_len, dv), lambda b, h, i: (b, h)),
            pl.BlockSpec((4, DIFF_DK), lambda b, h, i: (0, 0)),
            pl.BlockSpec((1, dv), lambda b, h, i: (0, 0)),
        ],
        out_specs=pl.BlockSpec((tq, dv), lambda b, h, i: (b * nq + i, h)),
        out_shape=jax.ShapeDtypeStruct((m, DIFF_H * dv), BF16),
        scratch_shapes=[pltpu.VMEM((n_chains, tqs, dv), F32), pltpu.VMEM((n_chains, tqs, LANES), F32),
                        pltpu.VMEM((n_chains, tqs, LANES), F32)],
        compiler_params=_cparams(("arbitrary", "arbitrary", "arbitrary")),
        name="diff_attn",
    )(q, k, v, lam4, subln_w)


def _attn_cached_kernel(q_ref, kn_ref, vn_ref, kc_ref, vc_ref, lam_ref, sw_ref, o_ref,
                        *, q_off, lam_init, n_heads):
    t = q_ref.shape[0]
    rows = kc_ref.shape[0]
    past = rows // n_heads
    dk = DIFF_DK
    dv = DIFF_DV
    shift = int(math.log2(CHUNK))
    head_shift = int(math.log2(n_heads))
    q_chunk = (q_off + lax.broadcasted_iota(jnp.int32, (t, 1), 0)) >> shift
    col = lax.broadcasted_iota(jnp.int32, (t, rows), 1)
    col_head = col & (n_heads - 1)
    vis_pos = ((col >> head_shift) >> shift) <= q_chunk
    vis_n = ((past + lax.broadcasted_iota(jnp.int32, (t, t), 1)) >> shift) <= q_chunk
    lam = (jnp.exp(jnp.sum(lam_ref[0:1, :] * lam_ref[1:2, :], axis=-1, keepdims=True))
           - jnp.exp(jnp.sum(lam_ref[2:3, :] * lam_ref[3:4, :], axis=-1, keepdims=True)) + lam_init)
    nt = (((1,), (1,)), ((), ()))
    kc = kc_ref[...].astype(BF16)
    vc = vc_ref[...].astype(BF16)
    for h in range(n_heads):
        vis_c = vis_pos & (col_head == h)
        hs = slice(h * dv, (h + 1) * dv)
        q = q_ref[:, hs]
        kn = kn_ref[:, hs]
        vn = vn_ref[:, hs]
        outs = []
        for c in range(2):
            cs = slice(c * dk, (c + 1) * dk)
            sc = jnp.where(vis_c, lax.dot_general(q[:, cs], kc[:, cs], nt, preferred_element_type=F32), NEG_BIG)
            sn = jnp.where(vis_n, lax.dot_general(q[:, cs], kn[:, cs], nt, preferred_element_type=F32), NEG_BIG)
            m = jnp.maximum(jnp.max(sc, axis=-1, keepdims=True), jnp.max(sn, axis=-1, keepdims=True))
            pc = jnp.exp2(sc - m)
            pn = jnp.exp2(sn - m)
            l = jnp.sum(pc, axis=-1, keepdims=True) + jnp.sum(pn, axis=-1, keepdims=True)
            acc = (jnp.dot(pc.astype(BF16), vc, preferred_element_type=F32)
                   + jnp.dot(pn.astype(BF16), vn, preferred_element_type=F32))
            outs.append(acc / l)
        o = outs[0] - lam * outs[1]
        o = (o * lax.rsqrt(jnp.mean(o * o, axis=-1, keepdims=True) + EPS)) * sw_ref[...]
        o_ref[:, hs] = (o * (1.0 - lam_init)).astype(o_ref.dtype)


def _attn_cached(q, k_new, v_new, k_cache, v_cache, lam4, subln_w, nb, t_len, q_off, lam_init):
    past, n_heads, dv = k_cache.shape[1:]
    assert n_heads & (n_heads - 1) == 0 and past >= 1
    width = n_heads * dv
    rows = past * n_heads
    seq = lambda b: (b, 0)
    return pl.pallas_call(
        functools.partial(_attn_cached_kernel, q_off=q_off, lam_init=lam_init, n_heads=n_heads),
        grid=(nb,),
        in_specs=[
            pl.BlockSpec((t_len, width), seq),
            pl.BlockSpec((t_len, width), seq),
            pl.BlockSpec((t_len, width), seq),
            pl.BlockSpec((rows, dv), seq),
            pl.BlockSpec((rows, dv), seq),
            pl.BlockSpec((4, DIFF_DK), lambda b: (0, 0)),
            pl.BlockSpec((1, dv), lambda b: (0, 0)),
        ],
        out_specs=pl.BlockSpec((t_len, width), seq),
        out_shape=jax.ShapeDtypeStruct((nb * t_len, width), BF16),
        compiler_params=_cparams(("arbitrary",)),
        name="diff_attn_cached",
    )(q, k_new, v_new, k_cache.reshape(nb * rows, dv), v_cache.reshape(nb * rows, dv), lam4, subln_w)


def _merge_kernel(ya_ref, yb_ref, ga_ref, gb_ref, wa_ref, wb_ref, o_ref):
    pa = jnp.dot(ya_ref[...], wa_ref[...], preferred_element_type=F32)
    pb = jnp.dot(yb_ref[...], wb_ref[...], preferred_element_type=F32)
    o_ref[...] = (_sigmoid(ga_ref[...]) * pa + _sigmoid(gb_ref[...]) * pb).astype(o_ref.dtype)


def _merge(ya, yb, proj, wa, wb, off_ga, off_gb, tm, tn):
    m = ya.shape[0]
    n = wa.shape[1]
    return pl.pallas_call(
        _merge_kernel,
        grid=(m // tm, n // tn),
        in_specs=[
            pl.BlockSpec((tm, ya.shape[1]), lambda i, j: (i, 0)),
            pl.BlockSpec((tm, yb.shape[1]), lambda i, j: (i, 0)),
            pl.BlockSpec((tm, tn), lambda i, j: (i, off_ga // tn + j)),
            pl.BlockSpec((tm, tn), lambda i, j: (i, off_gb // tn + j)),
            pl.BlockSpec((wa.shape[0], tn), lambda i, j: (0, j)),
            pl.BlockSpec((wb.shape[0], tn), lambda i, j: (0, j)),
        ],
        out_specs=pl.BlockSpec((tm, tn), lambda i, j: (i, j)),
        out_shape=jax.ShapeDtypeStruct((m, n), BF16),
        compiler_params=_cparams(("arbitrary", "arbitrary")),
        name="merge",
    )(ya, yb, proj, proj, wa, wb)


def _outproj_kernel(mg_ref, x_ref, w_ref, o_ref):
    o_ref[...] = x_ref[...] + jnp.dot(mg_ref[...], w_ref[...], preferred_element_type=F32)


def _outproj(merged, x, w, tm, tn):
    m, d = x.shape
    return pl.pallas_call(
        _outproj_kernel,
        grid=(m // tm, d // tn),
        in_specs=[
            pl.BlockSpec((tm, merged.shape[1]), lambda i, j: (i, 0)),
            pl.BlockSpec((tm, tn), lambda i, j: (i, j)),
            pl.BlockSpec((w.shape[0], tn), lambda i, j: (0, j)),
        ],
        out_specs=pl.BlockSpec((tm, tn), lambda i, j: (i, j)),
        out_shape=jax.ShapeDtypeStruct((m, d), F32),
        compiler_params=_cparams(("arbitrary", "arbitrary")),
        name="out_proj",
    )(merged, x, w)


def _ffn_kernel(h_ref, nw_ref, wg_ref, wu_ref, cw_ref, cb_ref, wd_ref, hist_ref, nf_ref,
                y_ref, tail_ref, hn_scr, acc_scr, carry_scr, *, n_seq, tiles_per_seq,
                final_norm):
    i = pl.program_id(0)
    j = pl.program_id(1)
    nj = pl.num_programs(1)
    tm = h_ref.shape[0]
    tf = wg_ref.shape[1]
    ts = tm // n_seq

    @pl.when(j == 0)
    def _():
        hh = h_ref[...]
        ms = jnp.mean(hh * hh, axis=-1, keepdims=True)
        hn_scr[...] = ((hh * lax.rsqrt(ms + EPS)) * nw_ref[...]).astype(BF16)
        acc_scr[...] = jnp.zeros(acc_scr.shape, F32)

    hn = hn_scr[...]
    g = jnp.dot(hn, wg_ref[...], preferred_element_type=F32)
    up = jnp.dot(hn, wu_ref[...], preferred_element_type=F32)

    @pl.when((i % tiles_per_seq) == 0)
    def _():
        carry_scr[j] = hist_ref[...]

    w0 = cw_ref[0:1, :]
    w1 = cw_ref[1:2, :]
    w2 = cw_ref[2:3, :]
    cb = cb_ref[...]
    outs = []
    for s in range(n_seq):
        gs = g[s * ts:(s + 1) * ts]
        prev = carry_scr[j, s]
        conv = gs * w2 + _shift_rows(gs, prev, 1) * w1 + _shift_rows(gs, prev, 2) * w0 + cb
        outs.append(_silu(conv) * up[s * ts:(s + 1) * ts])
        tail = gs[ts - SUBLANES:]
        carry_scr[j, s] = tail
        tail_ref[s] = tail
    act = outs[0] if n_seq == 1 else jnp.concatenate(outs, axis=0)
    acc_scr[...] += jnp.dot(act.astype(BF16), wd_ref[...], preferred_element_type=F32)

    @pl.when(j == nj - 1)
    def _():
        o = h_ref[...] + acc_scr[...]
        if final_norm:
            ms = jnp.mean(o * o, axis=-1, keepdims=True)
            o = (o * lax.rsqrt(ms + EPS)) * nf_ref[...]
        y_ref[...] = o


def _ffn(h, norm_w, wg, wu, conv_w, conv_b, wd, hist8, norm_final, final_norm, t_len, tm, tf):
    m, d = h.shape
    f = wg.shape[1]
    nj = f // tf
    if tm >= t_len:
        n_seq, tiles_per_seq = tm // t_len, 1
    else:
        n_seq, tiles_per_seq = 1, t_len // tm
    n_tiles = m // tm
    seq_blk = lambda i, j: ((i // tiles_per_seq), 0, j)
    y, tails = pl.pallas_call(
        functools.partial(_ffn_kernel, n_seq=n_seq, tiles_per_seq=tiles_per_seq, final_norm=final_norm),
        grid=(n_tiles, nj),
        in_specs=[
            pl.BlockSpec((tm, d), lambda i, j: (i, 0)),
            pl.BlockSpec((1, d), lambda i, j: (0, 0)),
            pl.BlockSpec((d, tf), lambda i, j: (0, j)),
            pl.BlockSpec((d, tf), lambda i, j: (0, j)),
            pl.BlockSpec((FFN_CONV, tf), lambda i, j: (0, j)),
            pl.BlockSpec((1, tf), lambda i, j: (0, j)),
            pl.BlockSpec((tf, d), lambda i, j: (j, 0)),
            pl.BlockSpec((n_seq, SUBLANES, tf), seq_blk),
            pl.BlockSpec((1, d), lambda i, j: (0, 0)),
        ],
        out_specs=[
            pl.BlockSpec((tm, d), lambda i, j: (i, 0)),
            pl.BlockSpec((n_seq, SUBLANES, tf), lambda i, j: (i, 0, j)),
        ],
        out_shape=[jax.ShapeDtypeStruct((m, d), F32),
                   jax.ShapeDtypeStruct((n_tiles * n_seq, SUBLANES, f), F32)],
        scratch_shapes=[pltpu.VMEM((tm, d), BF16), pltpu.VMEM((tm, d), F32),
                        pltpu.VMEM((nj, n_seq, SUBLANES, tf), F32)],
        compiler_params=_cparams(("arbitrary", "arbitrary")),
        name="conv_ffn",
    )(h, norm_w, wg, wu, conv_w, conv_b, wd, hist8, norm_final)
    return y, tails


def _pad_hist(hist):
    b, r, c = hist.shape
    return jnp.concatenate([jnp.zeros((b, SUBLANES - r, c), hist.dtype), hist], axis=1)


def _pick(n, cands):
    for c in cands:
        if n % c == 0:
            return c
    return n


def _layer(x, pos_offset, k_hist, v_hist, gdn_conv_hist, gdn_s0, ffn_hist, lam_init, wts, norm_final,
           final_norm):
    nb, t_len, d_model = x.shape
    m = nb * t_len
    past = k_hist.shape[1]
    x2 = x.reshape(m, d_model)
    qk_dim = GDN_HK * GDN_D
    v_dim = GDN_HV * GDN_D
    dqk = DIFF_H * 2 * DIFF_DK
    off = {"gq": 0, "gk": qk_dim, "gv": 2 * qk_dim, "gz": 2 * qk_dim + v_dim}
    off["dq"] = off["gz"] + v_dim
    off["dk"] = off["dq"] + dqk
    off["dv"] = off["dk"] + dqk
    off["ga"] = off["dv"] + DIFF_H * DIFF_DV
    off["gb"] = off["ga"] + d_model

    tm = _pick(m, (1024, 512, 256))
    tn = _pick(wts["w_main"].shape[1], (1024, 512, 256, 128))
    proj, gates = _inproj(x2, wts["norm_mix"], wts["w_main"], wts["w_gates"], tm, tn)

    tmr = _pick(m, (256,))
    pos = pos_offset + jnp.arange(t_len, dtype=jnp.int32)
    tables = _rope_tables(pos)
    if t_len < tmr:
        tables = [jnp.tile(tb, (tmr // t_len, 1)) for tb in tables]
    n_tab = max(t_len // tmr, 1)
    q_r, k_rows, k_b, v_rows, v_b = _rope(proj, tables, tmr, n_tab, off["dq"], off["dk"], off["dv"], dqk)

    chunk = CHUNK if t_len % CHUNK == 0 else t_len
    tt = _pick(t_len, (256,)) if t_len % CHUNK == 0 else t_len
    kg = _pick(GDN_HK, (4, 2))
    ya, gdn_s = _gdn(proj, gates, _pad_hist(gdn_conv_hist), wts["gdn_conv_w"], wts["alog"], wts["dtb"],
                     wts["gdn_norm_w"], gdn_s0, nb, t_len, tt, chunk, kg)
    qkv_tail = proj.reshape(nb, t_len, -1)[:, t_len - (GDN_CONV - 1):, :off["gz"]]
    if t_len >= GDN_CONV - 1:
        gdn_conv_new = qkv_tail
    else:
        gdn_conv_new = jnp.concatenate([gdn_conv_hist, proj.reshape(nb, t_len, -1)[:, :, :off["gz"]]],
                                       axis=1)[:, -(GDN_CONV - 1):]

    if past:
        yb = _attn_cached(q_r, k_b, v_b, k_hist, v_hist, wts["lam4"], wts["subln_w"], nb, t_len,
                          pos_offset, lam_init)
    else:
        tq = _pick(t_len, (512, 256))
        tqs = _pick(tq, (256,))
        tk = _pick(t_len, (512, 256))
        yb = _attn(q_r, k_b, v_b, wts["lam4"], wts["subln_w"], nb, t_len, t_len, tq, tqs, tk,
                   pos_offset, lam_init)

    tmm = _pick(m, (1024, 512, 256))
    tnm = _pick(d_model, (512, 256, 128))
    merged = _merge(ya, yb, proj, wts["w_br_a"], wts["w_br_b"], off["ga"], off["gb"], tmm, tnm)
    h = _outproj(merged, x2, wts["w_out"], tmm, tnm)
    tmf = _pick(m, (512, 256))
    y, tails = _ffn(h, wts["norm_ffn"], wts["w_gate"], wts["w_up"], wts["ffn_conv_w"], wts["ffn_conv_b"],
                    wts["w_down"], _pad_hist(ffn_hist), norm_final, final_norm, t_len, tmf, 512)
    d_ff = tails.shape[-1]
    if tmf >= t_len:
        seq_tails = tails
    else:
        seq_tails = tails.reshape(nb, t_len // tmf, SUBLANES, d_ff)[:, -1]
    ffn_new = seq_tails[:, SUBLANES - (FFN_CONV - 1):]

    return (y.reshape(nb, t_len, d_model),
            k_rows.reshape(nb, t_len, DIFF_H, 2 * DIFF_DK),
            v_rows.reshape(nb, t_len, DIFF_H, DIFF_DV),
            gdn_conv_new, gdn_s, ffn_new)


def _prep_weights(l, norm_mix, w_in, gdn_conv_w, gdn_a_log, gdn_dt_bias, gdn_norm_w,
                  lq1, lk1, lq2, lk2, subln_w, w_branch, w_out, norm_ffn,
                  w_gate, w_up, ffn_conv_w, ffn_conv_b, w_down):
    qk_dim = GDN_HK * GDN_D
    v_dim = GDN_HV * GDN_D
    g0 = 2 * qk_dim + 2 * v_dim
    g1 = g0 + 2 * GDN_HV
    wi = w_in[l].astype(BF16)
    d_model = wi.shape[0]
    w_main = jnp.concatenate([wi[:, :g0], wi[:, g1:]], axis=1)
    w_gates = jnp.concatenate([wi[:, g0:g1], jnp.zeros((d_model, LANES - 2 * GDN_HV), BF16)], axis=1)
    pad = jnp.zeros((LANES - GDN_HV,), F32)
    return {
        "norm_mix": norm_mix[l].reshape(1, -1),
        "w_main": w_main,
        "w_gates": w_gates,
        "gdn_conv_w": gdn_conv_w[l],
        "alog": jnp.concatenate([gdn_a_log[l].astype(F32), pad]).reshape(1, LANES),
        "dtb": jnp.concatenate([gdn_dt_bias[l].astype(F32), pad]).reshape(1, LANES),
        "gdn_norm_w": gdn_norm_w[l].reshape(1, -1),
        "lam4": jnp.stack([lq1[l], lk1[l], lq2[l], lk2[l]], axis=0).astype(F32),
        "subln_w": subln_w[l].reshape(1, -1),
        "w_br_a": w_branch[l][:v_dim].astype(BF16),
        "w_br_b": w_branch[l][v_dim:].astype(BF16),
        "w_out": w_out[l].astype(BF16),
        "norm_ffn": norm_ffn[l].reshape(1, -1),
        "w_gate": w_gate[l].astype(BF16),
        "w_up": w_up[l].astype(BF16),
        "ffn_conv_w": ffn_conv_w[l],
        "ffn_conv_b": ffn_conv_b[l].reshape(1, -1),
        "w_down": w_down[l].astype(BF16),
    }


def kernel(x_prompt, x_sample, cache_diff_k, cache_diff_v, state_gdn_conv, state_gdn_rec, state_ffn_conv,
           norm_mix, w_in, gdn_conv_w, gdn_a_log, gdn_dt_bias, gdn_norm_w,
           diff_lambda_q1, diff_lambda_k1, diff_lambda_q2, diff_lambda_k2, diff_subln_w,
           w_branch, w_out, norm_ffn, ffn_w_gate, ffn_w_up, ffn_conv_w, ffn_conv_b, ffn_w_down,
           norm_final):
    depth = w_in.shape[0]
    bp = x_prompt.shape[0]
    past = cache_diff_k.shape[2]
    dt = x_prompt.dtype
    d_ff = ffn_w_gate.shape[-1]
    conv_dim = gdn_conv_w.shape[-1]
    nf = norm_final.reshape(1, -1)
    hp, hs = x_prompt, x_sample
    p_out = [[], [], [], [], []]
    s_out = [[], [], [], [], []]
    for l in range(depth):
        lam_init = 0.8 - 0.6 * math.exp(-0.3 * l)
        wts = _prep_weights(l, norm_mix, w_in, gdn_conv_w, gdn_a_log, gdn_dt_bias, gdn_norm_w,
                            diff_lambda_q1, diff_lambda_k1, diff_lambda_q2, diff_lambda_k2, diff_subln_w,
                            w_branch, w_out, norm_ffn, ffn_w_gate, ffn_w_up, ffn_conv_w, ffn_conv_b,
                            ffn_w_down)
        last = l == depth - 1
        hp, *sp = _layer(hp, 0,
                         jnp.zeros((bp, 0, DIFF_H, 2 * DIFF_DK), dt),
                         jnp.zeros((bp, 0, DIFF_H, DIFF_DV), dt),
                         jnp.zeros((bp, GDN_CONV - 1, conv_dim), dt),
                         jnp.zeros((bp, GDN_HV, GDN_D, GDN_D), dt),
                         jnp.zeros((bp, FFN_CONV - 1, d_ff), dt),
                         lam_init, wts, nf, last)
        hs, *ss = _layer(hs, past, cache_diff_k[l], cache_diff_v[l], state_gdn_conv[l], state_gdn_rec[l],
                         state_ffn_conv[l], lam_init, wts, nf, last)
        for i in range(5):
            p_out[i].append(sp[i])
            s_out[i].append(ss[i])
    p_k, p_v, p_gdn_conv, p_gdn_rec, p_ffn_conv = [jnp.stack(a, axis=0) for a in p_out]
    s_k, s_v, s_gdn_conv, s_gdn_rec, s_ffn_conv = [jnp.stack(a, axis=0) for a in s_out]
    return (hp, hs, p_k, p_v, p_gdn_conv, p_gdn_rec, p_ffn_conv,
            s_k, s_v, s_gdn_conv, s_gdn_rec, s_ffn_conv)
```
